```python
import math
import jax, jax.numpy as jnp
from jax import lax
import numpy as np

D_MODEL = 4096
BATCH = 4
SEQ = 2048
DEPTH = 1

CHUNK = 64
LEFT_CHUNKS = 8
BAND = LEFT_CHUNKS + 1
HEAD_DIM = D_MODEL // 32
N_HEADS_A = 16
N_HEADS_B = 16
WIDTH_A = N_HEADS_A * HEAD_DIM
WIDTH_B = N_HEADS_B * HEAD_DIM
PROJ_WIDTH = 3 * WIDTH_A + 3 * WIDTH_B + 2 * D_MODEL
REL_CLIP = 256
QBLK = 128
N_EXPERTS = 32
TOP_K = 4
D_FF_EXPERT = D_MODEL // 2
SWIGLU_ALPHA = 1.702
SWIGLU_LIMIT = 7.0
MOE_BLOCK = 128
EPS = 1e-5
NEG = -1e30

kernel_name = "hybrid_chunked_stickbreaking_moe_block"


def rms_norm(x, g):
    xf = x.astype(jnp.float32)
    y = xf * lax.rsqrt(jnp.mean(xf * xf, axis=-1, keepdims=True) + EPS)
    return (y * g.astype(jnp.float32)).astype(x.dtype)


def chunk_attention(q, k, v, rel_bias):
    b_, s_, h_, d_ = q.shape
    nc = s_ // CHUNK
    qc = q.reshape(b_, nc, CHUNK, h_, d_)
    pad = ((0, 0), (LEFT_CHUNKS * CHUNK, 0), (0, 0), (0, 0))
    kp = jnp.pad(k, pad).reshape(b_, nc + LEFT_CHUNKS, CHUNK, h_, d_)
    vp = jnp.pad(v, pad).reshape(b_, nc + LEFT_CHUNKS, CHUNK, h_, d_)
    band = np.arange(nc)[:, None] + np.arange(BAND)[None, :]
    kb = kp[:, band]
    vb = vp[:, band]
    scores = jnp.einsum('bnqhd,bnjkhd->bhnqjk', qc, kb).astype(jnp.float32) * (1.0 / math.sqrt(d_))
    qi = np.arange(CHUNK)[:, None, None]
    ji = np.arange(BAND)[None, :, None]
    ki = np.arange(CHUNK)[None, None, :]
    rel = qi - ki + (LEFT_CHUNKS - ji) * CHUNK
    idx = np.clip(rel, -REL_CLIP, REL_CLIP) + REL_CLIP
    bias = jnp.transpose(rel_bias[idx].astype(jnp.float32), (3, 0, 1, 2))
    scores = scores + bias[None, :, None]
    valid = band >= LEFT_CHUNKS
    scores = jnp.where(valid[None, None, :, None, :, None], scores, NEG)
    sh = scores.shape
    p = jax.nn.softmax(scores.reshape(sh[:4] + (BAND * CHUNK,)), axis=-1).reshape(sh)
    o = jnp.einsum('bhnqjk,bnjkhd->bnqhd', p.astype(v.dtype), vb)
    return o.reshape(b_, s_, h_ * d_)


def stick_breaking_attention(q, k, v):
    b_, s_, h_, d_ = q.shape
    scale = 1.0 / math.sqrt(d_)
    outs = []
    for bi in range(s_ // QBLK):
        q0 = bi * QBLK
        end = q0 + QBLK
        z = jnp.einsum('bqhd,bkhd->bhqk', q[:, q0:end], k[:, :end]).astype(jnp.float32) * scale
        mask = np.arange(end)[None, :] < (q0 + np.arange(QBLK))[:, None]
        log_keep = jnp.where(mask, -jax.nn.softplus(z), 0.0)
        later = lax.cumsum(log_keep, axis=3, reverse=True) - log_keep
        a = jnp.where(mask, jnp.exp(jax.nn.log_sigmoid(z) + later), 0.0)
        outs.append(jnp.einsum('bhqk,bkhd->bqhd', a.astype(v.dtype), v[:, :end]))
    return jnp.concatenate(outs, axis=1).reshape(b_, s_, h_ * d_)


def mixer_block(xn, w_in, b_gate, rel_bias, w_branch_a, w_branch_b, w_out):
    b_, s_, _ = xn.shape
    proj = xn @ w_in
    splits = np.cumsum([WIDTH_A, WIDTH_A, WIDTH_A, WIDTH_B, WIDTH_B, WIDTH_B, D_MODEL])
    qa, ka, va, qb, kb, vb, ga, gb = jnp.split(proj, splits, axis=-1)
    ha = lambda t: t.reshape(b_, s_, N_HEADS_A, HEAD_DIM)
    hb = lambda t: t.reshape(b_, s_, N_HEADS_B, HEAD_DIM)
    ya = chunk_attention(ha(qa), ha(ka), ha(va), rel_bias) @ w_branch_a
    yb = stick_breaking_attention(hb(qb), hb(kb), hb(vb)) @ w_branch_b
    gates = jax.nn.sigmoid(jnp.concatenate([ga, gb], axis=-1) + b_gate)
    gate_a, gate_b = jnp.split(gates, 2, axis=-1)
    return (gate_a * ya + gate_b * yb) @ w_out


def moe_ffn(xn, w_router, b_router, w_gate_up, b_gate_up, w_down, b_down):
    b_, s_, d_ = xn.shape
    t_ = b_ * s_
    xt = xn.reshape(t_, d_)
    logits = (xt @ w_router + b_router).astype(jnp.float32)
    top_vals, top_idx = lax.top_k(logits, TOP_K)
    top_w = jax.nn.softmax(top_vals, axis=-1)
    n_assign = t_ * TOP_K
    e_flat = top_idx.reshape(-1)
    w_flat = top_w.reshape(-1)
    tok_flat = jnp.arange(n_assign, dtype=jnp.int32) // TOP_K
    order = jnp.argsort(e_flat)
    e_sorted = e_flat[order]
    counts = jnp.bincount(e_flat, length=N_EXPERTS)
    padded = (counts + MOE_BLOCK - 1) // MOE_BLOCK * MOE_BLOCK
    group_start = jnp.cumsum(counts) - counts
    padded_end = jnp.cumsum(padded)
    padded_start = padded_end - padded
    dest = padded_start[e_sorted] + jnp.arange(n_assign) - group_start[e_sorted]
    n_blocks = -(-n_assign // MOE_BLOCK) + N_EXPERTS
    n_rows = n_blocks * MOE_BLOCK
    row_tok = jnp.zeros((n_rows,), jnp.int32).at[dest].set(tok_flat[order])
    row_w = jnp.zeros((n_rows,), jnp.float32).at[dest].set(w_flat[order])
    block_expert = jnp.minimum(
        jnp.searchsorted(padded_end, jnp.arange(n_blocks) * MOE_BLOCK, side='right'), N_EXPERTS - 1)
    xs = xt[row_tok].reshape(n_blocks, MOE_BLOCK, d_)

    def expert_block(args):
        xb, e = args
        h = xb @ w_gate_up[e] + b_gate_up[e]
        glu = jnp.minimum(h[:, :D_FF_EXPERT], SWIGLU_LIMIT)
        lin = jnp.clip(h[:, D_FF_EXPERT:], -SWIGLU_LIMIT, SWIGLU_LIMIT)
        act = glu * jax.nn.sigmoid(SWIGLU_ALPHA * glu) * (lin + 1.0)
        return act @ w_down[e] + b_down[e]

    ys = lax.map(expert_block, (xs, block_expert)).reshape(n_rows, d_)
    y = jnp.zeros((t_, d_), xn.dtype).at[row_tok].add(ys * row_w[:, None].astype(ys.dtype))
    return y.reshape(b_, s_, d_)


def setup_inputs(seed: int = 0) -> dict:
    key = jax.random.key(seed)
    ks = jax.random.split(key, 16)
    f32 = jnp.float32
    nrm = lambda k, shape, scale: jax.random.normal(k, shape, f32) * scale
    return {
        "x": nrm(ks[0], (BATCH, SEQ, D_MODEL), 1.0),
        "norm_mix_g": 1.0 + nrm(ks[1], (D_MODEL,), 0.02),
        "w_in": nrm(ks[2], (D_MODEL, PROJ_WIDTH), D_MODEL ** -0.5),
        "b_gate": nrm(ks[3], (2 * D_MODEL,), 0.02),
        "rel_bias": nrm(ks[4], (2 * REL_CLIP + 1, N_HEADS_A), 0.1),
        "w_branch_a": nrm(ks[5], (WIDTH_A, D_MODEL), WIDTH_A ** -0.5),
        "w_branch_b": nrm(ks[6], (WIDTH_B, D_MODEL), WIDTH_B ** -0.5),
        "w_out": nrm(ks[7], (D_MODEL, D_MODEL), D_MODEL ** -0.5),
        "norm_ffn_g": 1.0 + nrm(ks[8], (D_MODEL,), 0.02),
        "w_router": nrm(ks[9], (D_MODEL, N_EXPERTS), D_MODEL ** -0.5),
        "b_router": nrm(ks[10], (N_EXPERTS,), 0.01),
        "w_gate_up": nrm(ks[11], (N_EXPERTS, D_MODEL, 2 * D_FF_EXPERT), D_MODEL ** -0.5),
        "b_gate_up": nrm(ks[12], (N_EXPERTS, 2 * D_FF_EXPERT), 0.02),
        "w_down": nrm(ks[13], (N_EXPERTS, D_FF_EXPERT, D_MODEL), D_FF_EXPERT ** -0.5),
        "b_down": nrm(ks[14], (N_EXPERTS, D_MODEL), 0.02),
        "norm_final_g": 1.0 + nrm(ks[15], (D_MODEL,), 0.02),
    }


def reference(x, norm_mix_g, w_in, b_gate, rel_bias, w_branch_a, w_branch_b, w_out,
              norm_ffn_g, w_router, b_router, w_gate_up, b_gate_up, w_down, b_down,
              norm_final_g):
    h = x
    for _ in range(DEPTH):
        h = h + mixer_block(rms_norm(h, norm_mix_g), w_in, b_gate, rel_bias,
                            w_branch_a, w_branch_b, w_out)
        h = h + moe_ffn(rms_norm(h, norm_ffn_g), w_router, b_router,
                        w_gate_up, b_gate_up, w_down, b_down)
    return rms_norm(h, norm_final_g)
```

```python
import functools
import math

import jax
import jax.numpy as jnp
import numpy as np
from jax import lax
from jax.experimental import pallas as pl
from jax.experimental.pallas import tpu as pltpu

F32 = jnp.float32
BF16 = jnp.bfloat16

HEAD_DIM = 128
N_HEADS_A = 16
N_HEADS_B = 16
CHUNK = 64
LEFT_CHUNKS = 8
REL_CLIP = 256
N_EXPERTS = 32
TOP_K = 4
SWIGLU_ALPHA = 1.702
SWIGLU_LIMIT = 7.0
EPS = 1e-5
NEG = -1e30

V7X_LANES = 128
V7X_VMEM_BYTES = 64 * 1024 * 1024
VMEM_LIMIT = V7X_VMEM_BYTES - 8 * 1024 * 1024


def _cparams(n_axes):
    return pltpu.CompilerParams(
        dimension_semantics=("arbitrary",) * n_axes, vmem_limit_bytes=VMEM_LIMIT)


def _rmsnorm_kernel(x_ref, g_ref, o_ref):
    x = x_ref[...]
    ms = jnp.mean(x * x, axis=-1, keepdims=True)
    o_ref[...] = (x * lax.rsqrt(ms + EPS) * g_ref[...]).astype(o_ref.dtype)


def _rmsnorm(x, g, *, tm, out_dtype):
    t, d = x.shape
    return pl.pallas_call(
        _rmsnorm_kernel,
        grid=(t // tm,),
        in_specs=[pl.BlockSpec((tm, d), lambda i: (i, 0)),
                  pl.BlockSpec((1, d), lambda i: (0, 0))],
        out_specs=pl.BlockSpec((tm, d), lambda i: (i, 0)),
        out_shape=jax.ShapeDtypeStruct((t, d), out_dtype),
        compiler_params=_cparams(1),
        name="rmsnorm",
    )(x, g.reshape(1, d))


def _mm_kernel(*refs, has_bias, has_res, act):
    a_ref, w_ref = refs[0], refs[1]
    pos = 2
    b_ref = r_ref = None
    if has_bias:
        b_ref = refs[pos]
        pos += 1
    if has_res:
        r_ref = refs[pos]
        pos += 1
    o_ref, wbf_ref = refs[pos], refs[pos + 1]

    @pl.when(pl.program_id(1) == 0)
    def _():
        wbf_ref[...] = w_ref[...].astype(BF16)

    acc = jnp.dot(a_ref[...], wbf_ref[...], preferred_element_type=F32)
    if has_bias:
        acc = acc + b_ref[...]
    if act == "sigmoid":
        acc = jax.nn.sigmoid(acc)
    if has_res:
        acc = acc + r_ref[...]
    o_ref[...] = acc.astype(o_ref.dtype)


def _mm(a, w, *, col_off, n_out, tm, tn, out_dtype, bias=None, res=None, act=None, name):
    m, k = a.shape
    assert w.shape[0] == k and col_off % tn == 0 and n_out % tn == 0 and m % tm == 0
    off = col_off // tn
    in_specs = [pl.BlockSpec((tm, k), lambda j, i: (i, 0)),
                pl.BlockSpec((k, tn), lambda j, i: (0, j + off))]
    args = [a, w]
    if bias is not None:
        in_specs.append(pl.BlockSpec((1, tn), lambda j, i: (0, j)))
        args.append(bias.reshape(1, n_out))
    if res is not None:
        in_specs.append(pl.BlockSpec((tm, tn), lambda j, i: (i, j)))
        args.append(res)
    kern = functools.partial(_mm_kernel, has_bias=bias is not None, has_res=res is not None, act=act)
    return pl.pallas_call(
        kern,
        grid=(n_out // tn, m // tm),
        in_specs=in_specs,
        out_specs=pl.BlockSpec((tm, tn), lambda j, i: (i, j)),
        out_shape=jax.ShapeDtypeStruct((m, n_out), out_dtype),
        scratch_shapes=[pltpu.VMEM((k, tn), BF16)],
        compiler_params=_cparams(2),
        name=name,
    )(*args)


def _branch_kernel(a_ref, b_ref, wa_ref, wb_ref, ga_ref, gb_ref, o_ref, wa_bf, wb_bf):
    @pl.when(pl.program_id(1) == 0)
    def _():
        wa_bf[...] = wa_ref[...].astype(BF16)
        wb_bf[...] = wb_ref[...].astype(BF16)

    ya = jnp.dot(a_ref[...], wa_bf[...], preferred_element_type=F32)
    yb = jnp.dot(b_ref[...], wb_bf[...], preferred_element_type=F32)
    o_ref[...] = (ga_ref[...].astype(F32) * ya + gb_ref[...].astype(F32) * yb).astype(o_ref.dtype)


def _branch_merge(att_a, att_b, wa, wb, gates, *, tm, tn):
    m, ka = att_a.shape
    kb = att_b.shape[1]
    d = wa.shape[1]
    nj = d // tn
    return pl.pallas_call(
        _branch_kernel,
        grid=(nj, m // tm),
        in_specs=[pl.BlockSpec((tm, ka), lambda j, i: (i, 0)),
                  pl.BlockSpec((tm, kb), lambda j, i: (i, 0)),
                  pl.BlockSpec((ka, tn), lambda j, i: (0, j)),
                  pl.BlockSpec((kb, tn), lambda j, i: (0, j)),
                  pl.BlockSpec((tm, tn), lambda j, i: (i, j)),
                  pl.BlockSpec((tm, tn), lambda j, i: (i, j + nj))],
        out_specs=pl.BlockSpec((tm, tn), lambda j, i: (i, j)),
        out_shape=jax.ShapeDtypeStruct((m, d), BF16),
        scratch_shapes=[pltpu.VMEM((ka, tn), BF16), pltpu.VMEM((kb, tn), BF16)],
        compiler_params=_cparams(2),
        name="branch_merge",
    )(att_a, att_b, wa, wb, gates, gates)


def _chunk_bias_table(rel_bias, tq):
    left = LEFT_CHUNKS * CHUNK
    assert left == 2 * tq
    qi = np.arange(tq)[:, None]
    kw = np.arange(3 * tq)[None, :]
    krel = kw - left
    rel = qi - krel
    qc = qi // CHUNK
    kc = np.floor_divide(krel, CHUNK)
    valid = (kc >= qc - LEFT_CHUNKS) & (kc <= qc)
    idx = np.clip(rel, -REL_CLIP, REL_CLIP) + REL_CLIP
    bias = jnp.transpose(rel_bias[idx].astype(F32), (2, 0, 1))
    return jnp.where(jnp.asarray(valid)[None], bias, NEG)


def _chunk_attn_kernel(q_ref, k_ref, v_ref, bias_ref, o_ref, *, tq, scale):
    i = pl.program_id(2)
    q = q_ref[...]
    s_blocks = []
    v_blocks = []
    for j in range(3):
        kb = i - 2 + j
        start = pl.multiple_of(jnp.maximum(kb, 0) * tq, tq)
        k = k_ref[pl.ds(start, tq), :]
        v_blocks.append(v_ref[pl.ds(start, tq), :])
        s = lax.dot_general(q, k, (((1,), (1,)), ((), ())), preferred_element_type=F32) * scale
        s = s + bias_ref[0, :, j * tq:(j + 1) * tq]
        if j < 2:
            s = jnp.where(kb >= 0, s, NEG)
        s_blocks.append(s)
    m = jnp.maximum(jnp.maximum(jnp.max(s_blocks[0], axis=-1, keepdims=True),
                                jnp.max(s_blocks[1], axis=-1, keepdims=True)),
                    jnp.max(s_blocks[2], axis=-1, keepdims=True))
    l = jnp.zeros_like(m)
    acc = jnp.zeros((tq, HEAD_DIM), F32)
    for j in range(3):
        p = jnp.exp(s_blocks[j] - m)
        l = l + jnp.sum(p, axis=-1, keepdims=True)
        acc = acc + jnp.dot(p.astype(BF16), v_blocks[j], preferred_element_type=F32)
    o_ref[...] = (acc / l).astype(o_ref.dtype)


def _chunk_attention(qkv, bias_tab, *, batch, seq, tq):
    nq = seq // tq
    nh = N_HEADS_A
    kern = functools.partial(_chunk_attn_kernel, tq=tq, scale=1.0 / math.sqrt(HEAD_DIM))
    return pl.pallas_call(
        kern,
        grid=(batch, nh, nq),
        in_specs=[pl.BlockSpec((tq, HEAD_DIM), lambda b, h, i: (b * nq + i, h)),
                  pl.BlockSpec((seq, HEAD_DIM), lambda b, h, i: (b, nh + h)),
                  pl.BlockSpec((seq, HEAD_DIM), lambda b, h, i: (b, 2 * nh + h)),
                  pl.BlockSpec((1, tq, 3 * tq), lambda b, h, i: (h, 0, 0))],
        out_specs=pl.BlockSpec((tq, HEAD_DIM), lambda b, h, i: (b * nq + i, h)),
        out_shape=jax.ShapeDtypeStruct((batch * seq, nh * HEAD_DIM), BF16),
        compiler_params=_cparams(3),
        name="chunk_attention",
    )(qkv, qkv, qkv, bias_tab)


def _suffix_matrix(tk):
    j = np.arange(tk)[:, None]
    s = np.arange(tk)[None, :]
    later = (j > s).astype(np.float32)
    half = np.concatenate([later, np.ones((tk, tk), np.float32)], axis=1)
    return jnp.asarray(np.concatenate([half, half], axis=0), BF16)


def _softplus(z):
    return jnp.maximum(z, 0.0) + jnp.log1p(jnp.exp(-jnp.abs(z)))


def _stick_kernel(q_ref, k_ref, v_ref, u_ref, o_ref, *, tq, tk, scale):
    i = pl.program_id(2)
    q = q_ref[...]
    u = u_ref[...]
    n_kb = (i + 1) * (tq // tk)
    qpos = i * tq + lax.broadcasted_iota(jnp.int32, (tq, tk), 0)
    kcol = lax.broadcasted_iota(jnp.int32, (tq, tk), 1)

    def body(t, carry):
        acc, run = carry
        start = pl.multiple_of((n_kb - 1 - t) * tk, tk)
        k = k_ref[pl.ds(start, tk), :]
        v = v_ref[pl.ds(start, tk), :]
        z = lax.dot_general(q, k, (((1,), (1,)), ((), ())), preferred_element_type=F32) * scale
        mask = (kcol + start) < qpos
        sp = _softplus(z)
        log_keep = jnp.where(mask, -sp, 0.0)
        hi = log_keep.astype(BF16)
        lo = (log_keep - hi.astype(F32)).astype(BF16)
        sums = jnp.dot(jnp.concatenate([hi, lo], axis=1), u, preferred_element_type=F32)
        later = sums[:, :tk] + run
        a = jnp.where(mask, jnp.exp(z - sp + later), 0.0)
        acc = acc + jnp.dot(a.astype(BF16), v, preferred_element_type=F32)
        return acc, run + sums[:, tk:]

    zeros = jnp.zeros((tq, tk), F32)
    acc, _ = lax.fori_loop(0, n_kb, body, (jnp.zeros((tq, HEAD_DIM), F32), zeros))
    o_ref[...] = acc.astype(o_ref.dtype)


def _stick_attention(qkv, *, batch, seq, tq, tk, col_off):
    nq = seq // tq
    nh = N_HEADS_B
    c0 = col_off // HEAD_DIM
    kern = functools.partial(_stick_kernel, tq=tq, tk=tk, scale=1.0 / math.sqrt(HEAD_DIM))
    return pl.pallas_call(
        kern,
        grid=(batch, nh, nq),
        in_specs=[pl.BlockSpec((tq, HEAD_DIM), lambda b, h, i: (b * nq + i, c0 + h)),
                  pl.BlockSpec((seq, HEAD_DIM), lambda b, h, i: (b, c0 + nh + h)),
                  pl.BlockSpec((seq, HEAD_DIM), lambda b, h, i: (b, c0 + 2 * nh + h)),
                  pl.BlockSpec((2 * tk, 2 * tk), lambda b, h, i: (0, 0))],
        out_specs=pl.BlockSpec((tq, HEAD_DIM), lambda b, h, i: (b * nq + i, h)),
        out_shape=jax.ShapeDtypeStruct((batch * seq, nh * HEAD_DIM), BF16),
        compiler_params=_cparams(3),
        name="stick_attention",
    )(qkv, qkv, qkv, _suffix_matrix(tk))


def _router_kernel(h_ref, g_ref, wr_ref, br_ref, xn_ref, idx_ref, wt_ref, rank_ref, cnt_ref,
                   wr_hi, wr_lo, carry, *, tm):
    step = pl.program_id(0)

    @pl.when(step == 0)
    def _():
        w = wr_ref[...]
        hi = w.astype(BF16)
        wr_hi[...] = hi
        wr_lo[...] = (w - hi.astype(F32)).astype(BF16)
        carry[...] = jnp.zeros_like(carry)

    x = h_ref[...]
    ms = jnp.mean(x * x, axis=-1, keepdims=True)
    xn = x * lax.rsqrt(ms + EPS) * g_ref[...]
    xn_ref[...] = xn
    xh = xn.astype(BF16)
    xl = (xn - xh.astype(F32)).astype(BF16)
    logits = (jnp.dot(xh, wr_hi[...], preferred_element_type=F32)
              + jnp.dot(xh, wr_lo[...], preferred_element_type=F32)
              + jnp.dot(xl, wr_hi[...], preferred_element_type=F32)) + br_ref[...]
    lane = lax.broadcasted_iota(jnp.int32, (tm, V7X_LANES), 1)
    logits = jnp.where(lane < N_EXPERTS, logits, -jnp.inf)

    r = lax.broadcasted_iota(jnp.int32, (tm, tm), 0)
    c = lax.broadcasted_iota(jnp.int32, (tm, tm), 1)
    tri = (c < r).astype(BF16)

    sel_mask = jnp.zeros((tm, V7X_LANES), F32)
    onehots, vals = [], []
    work = logits
    for _ in range(TOP_K):
        mx = jnp.max(work, axis=-1, keepdims=True)
        first = jnp.min(jnp.where(work == mx, lane, V7X_LANES), axis=-1, keepdims=True)
        oh = lane == first
        onehots.append(oh)
        vals.append(mx)
        sel_mask = sel_mask + oh.astype(F32)
        work = jnp.where(oh, -jnp.inf, work)

    prefix = jnp.dot(tri, sel_mask.astype(BF16), preferred_element_type=F32) + carry[...]
    carry[...] = carry[...] + jnp.sum(sel_mask, axis=0, keepdims=True)
    cnt_ref[...] = jnp.broadcast_to(carry[...], cnt_ref.shape)

    es = [jnp.exp(v - vals[0]) for v in vals]
    denom = es[0] + es[1] + es[2] + es[3]
    idx_out = jnp.zeros((tm, V7X_LANES), jnp.int32)
    wt_out = jnp.zeros((tm, V7X_LANES), F32)
    rank_out = jnp.zeros((tm, V7X_LANES), jnp.int32)
    for k in range(TOP_K):
        oh = onehots[k]
        e_k = jnp.sum(jnp.where(oh, lane, 0), axis=-1, keepdims=True)
        r_k = jnp.sum(jnp.where(oh, prefix, 0.0), axis=-1, keepdims=True).astype(jnp.int32)
        idx_out = jnp.where(lane == k, e_k, idx_out)
        wt_out = jnp.where(lane == k, es[k] / denom, wt_out)
        rank_out = jnp.where(lane == k, r_k, rank_out)
    idx_ref[...] = idx_out
    wt_ref[...] = wt_out
    rank_ref[...] = rank_out


def _router(h, g, w_router, b_router, *, tm):
    t, d = h.shape
    wr = jnp.zeros((d, V7X_LANES), F32).at[:, :N_EXPERTS].set(w_router)
    br = jnp.zeros((1, V7X_LANES), F32).at[0, :N_EXPERTS].set(b_router)
    row = lambda i: (i, 0)
    fixed = lambda i: (0, 0)
    kern = functools.partial(_router_kernel, tm=tm)
    return pl.pallas_call(
        kern,
        grid=(t // tm,),
        in_specs=[pl.BlockSpec((tm, d), row),
                  pl.BlockSpec((1, d), fixed),
                  pl.BlockSpec((d, V7X_LANES), fixed),
                  pl.BlockSpec((1, V7X_LANES), fixed)],
        out_specs=[pl.BlockSpec((tm, d), row),
                   pl.BlockSpec((tm, V7X_LANES), row),
                   pl.BlockSpec((tm, V7X_LANES), row),
                   pl.BlockSpec((tm, V7X_LANES), row),
                   pl.BlockSpec((8, V7X_LANES), fixed)],
        out_shape=[jax.ShapeDtypeStruct((t, d), F32),
                   jax.ShapeDtypeStruct((t, V7X_LANES), jnp.int32),
                   jax.ShapeDtypeStruct((t, V7X_LANES), F32),
                   jax.ShapeDtypeStruct((t, V7X_LANES), jnp.int32),
                   jax.ShapeDtypeStruct((8, V7X_LANES), F32)],
        scratch_shapes=[pltpu.VMEM((d, V7X_LANES), BF16),
                        pltpu.VMEM((d, V7X_LANES), BF16),
                        pltpu.VMEM((1, V7X_LANES), F32)],
        compiler_params=_cparams(1),
        name="router",
    )(h, g.reshape(1, d), wr, br)


def _row_copy(src_hbm, dst, sem, src_row, dst_row):
    return pltpu.make_async_copy(src_hbm.at[pl.ds(src_row, 1)], dst.at[pl.ds(dst_row, 1)], sem)


def _gather_kernel(tok_ref, x_hbm, o_ref, buf, sem, *, bm):
    r0 = pl.program_id(0) * bm

    def issue(r, c):
        _row_copy(x_hbm, buf, sem, tok_ref[r0 + r], r).start()
        return c

    lax.fori_loop(0, bm, issue, 0)

    def drain(r, c):
        _row_copy(x_hbm, buf, sem, 0, r).wait()
        return c

    lax.fori_loop(0, bm, drain, 0)
    o_ref[...] = buf[...].astype(o_ref.dtype)


def _gather_rows(row_tok, xn, *, bm):
    n_rows = row_tok.shape[0]
    d = xn.shape[1]
    kern = functools.partial(_gather_kernel, bm=bm)
    return pl.pallas_call(
        kern,
        grid_spec=pltpu.PrefetchScalarGridSpec(
            num_scalar_prefetch=1,
            grid=(n_rows // bm,),
            in_specs=[pl.BlockSpec(memory_space=pl.ANY)],
            out_specs=pl.BlockSpec((bm, d), lambda i, tok: (i, 0)),
            scratch_shapes=[pltpu.VMEM((bm, d), F32), pltpu.SemaphoreType.DMA(())]),
        out_shape=jax.ShapeDtypeStruct((n_rows, d), BF16),
        compiler_params=_cparams(1),
        name="gather_rows",
    )(row_tok, xn)


def _expert_up_kernel(be_ref, nu_ref, xs_ref, wg_ref, wl_ref, bg_ref, bl_ref, h_ref, wg_bf, wl_bf):
    i = pl.program_id(1)

    @pl.when(i < nu_ref[0])
    def _():
        changed = jnp.logical_or(i == 0, be_ref[i] != be_ref[jnp.maximum(i - 1, 0)])

        @pl.when(changed)
        def _():
            wg_bf[...] = wg_ref[0].astype(BF16)
            wl_bf[...] = wl_ref[0].astype(BF16)

        x = xs_ref[...]
        g = jnp.dot(x, wg_bf[...], preferred_element_type=F32) + bg_ref[0]
        l = jnp.dot(x, wl_bf[...], preferred_element_type=F32) + bl_ref[0]
        glu = jnp.minimum(g, SWIGLU_LIMIT)
        lin = jnp.clip(l, -SWIGLU_LIMIT, SWIGLU_LIMIT)
        h_ref[...] = (glu * jax.nn.sigmoid(SWIGLU_ALPHA * glu) * (lin + 1.0)).astype(h_ref.dtype)

    @pl.when(i >= nu_ref[0])
    def _():
        h_ref[...] = jnp.zeros_like(h_ref)


def _expert_up(block_expert, n_used, xs, w_gate_up, b_gate_up, *, bm, tf):
    n_rows, d = xs.shape
    n_e, _, two_ff = w_gate_up.shape
    d_ff = two_ff // 2
    nj = d_ff // tf

    def blk(i, nu):
        return jnp.minimum(i, nu[0] - 1)

    return pl.pallas_call(
        _expert_up_kernel,
        grid_spec=pltpu.PrefetchScalarGridSpec(
            num_scalar_prefetch=2,
            grid=(nj, n_rows // bm),
            in_specs=[pl.BlockSpec((bm, d), lambda j, i, be, nu: (blk(i, nu), 0)),
                      pl.BlockSpec((1, d, tf), lambda j, i, be, nu: (be[blk(i, nu)], 0, j)),
                      pl.BlockSpec((1, d, tf), lambda j, i, be, nu: (be[blk(i, nu)], 0, nj + j)),
                      pl.BlockSpec((1, 1, tf), lambda j, i, be, nu: (be[blk(i, nu)], 0, j)),
                      pl.BlockSpec((1, 1, tf), lambda j, i, be, nu: (be[blk(i, nu)], 0, nj + j))],
            out_specs=pl.BlockSpec((bm, tf), lambda j, i, be, nu: (i, j)),
            scratch_shapes=[pltpu.VMEM((d, tf), BF16), pltpu.VMEM((d, tf), BF16)]),
        out_shape=jax.ShapeDtypeStruct((n_rows, d_ff), BF16),
        compiler_params=_cparams(2),
        name="expert_up",
    )(block_expert, n_used, xs, w_gate_up, w_gate_up,
      b_gate_up.reshape(n_e, 1, two_ff), b_gate_up.reshape(n_e, 1, two_ff))


def _expert_down_kernel(be_ref, nu_ref, h_ref, w_ref, b_ref, y_ref, w_bf):
    i = pl.program_id(1)

    @pl.when(i < nu_ref[0])
    def _():
        changed = jnp.logical_or(i == 0, be_ref[i] != be_ref[jnp.maximum(i - 1, 0)])

        @pl.when(changed)
        def _():
            w_bf[...] = w_ref[0].astype(BF16)

        y_ref[...] = (jnp.dot(h_ref[...], w_bf[...], preferred_element_type=F32)
                      + b_ref[0]).astype(y_ref.dtype)

    @pl.when(i >= nu_ref[0])
    def _():
        y_ref[...] = jnp.zeros_like(y_ref)


def _expert_down(block_expert, n_used, hidden, w_down, b_down, *, bm, tn):
    n_rows, d_ff = hidden.shape
    n_e, _, d = w_down.shape

    def blk(i, nu):
        return jnp.minimum(i, nu[0] - 1)

    return pl.pallas_call(
        _expert_down_kernel,
        grid_spec=pltpu.PrefetchScalarGridSpec(
            num_scalar_prefetch=2,
            grid=(d // tn, n_rows // bm),
            in_specs=[pl.BlockSpec((bm, d_ff), lambda j, i, be, nu: (blk(i, nu), 0)),
                      pl.BlockSpec((1, d_ff, tn), lambda j, i, be, nu: (be[blk(i, nu)], 0, j)),
                      pl.BlockSpec((1, 1, tn), lambda j, i, be, nu: (be[blk(i, nu)], 0, j))],
            out_specs=pl.BlockSpec((bm, tn), lambda j, i, be, nu: (i, j)),
            scratch_shapes=[pltpu.VMEM((d_ff, tn), BF16)]),
        out_shape=jax.ShapeDtypeStruct((n_rows, d), F32),
        compiler_params=_cparams(2),
        name="expert_down",
    )(block_expert, n_used, hidden, w_down, b_down.reshape(n_e, 1, d))


def _combine_kernel(dest_ref, h_ref, wt_ref, ys_hbm, g_ref, o_ref, buf, sem, *, tm):
    t0 = pl.program_id(0) * tm

    def issue(r, c):
        for k in range(TOP_K):
            _row_copy(ys_hbm, buf.at[k], sem, dest_ref[(t0 + r) * TOP_K + k], r).start()
        return c

    lax.fori_loop(0, tm, issue, 0)

    def drain(r, c):
        for k in range(TOP_K):
            _row_copy(ys_hbm, buf.at[k], sem, 0, r).wait()
        return c

    lax.fori_loop(0, tm, drain, 0)
    wt = wt_ref[...]
    y = h_ref[...]
    for k in range(TOP_K):
        y = y + wt[:, k:k + 1] * buf[k]
    ms = jnp.mean(y * y, axis=-1, keepdims=True)
    o_ref[...] = y * lax.rsqrt(ms + EPS) * g_ref[...]


def _combine(dest_flat, h, wts, ys, g, *, tm):
    t, d = h.shape
    kern = functools.partial(_combine_kernel, tm=tm)
    return pl.pallas_call(
        kern,
        grid_spec=pltpu.PrefetchScalarGridSpec(
            num_scalar_prefetch=1,
            grid=(t // tm,),
            in_specs=[pl.BlockSpec((tm, d), lambda i, dst: (i, 0)),
                      pl.BlockSpec((tm, V7X_LANES), lambda i, dst: (i, 0)),
                      pl.BlockSpec(memory_space=pl.ANY),
                      pl.BlockSpec((1, d), lambda i, dst: (0, 0))],
            out_specs=pl.BlockSpec((tm, d), lambda i, dst: (i, 0)),
            scratch_shapes=[pltpu.VMEM((TOP_K, tm, d), F32), pltpu.SemaphoreType.DMA(())]),
        out_shape=jax.ShapeDtypeStruct((t, d), F32),
        compiler_params=_cparams(1),
        name="combine",
    )(dest_flat, h, wts, ys, g.reshape(1, d))


class _Tiles:
    norm_rows = 256
    mm_rows = 1024
    mm_cols = 512
    chunk_q = 256
    stick_q = 256
    stick_k = 128
    router_rows = 256
    moe_block = 256
    up_cols = 512
    down_cols = 1024
    combine_rows = 128


def _moe_layout(idx, rank, counts, *, bm):
    t = idx.shape[0]
    n_assign = t * TOP_K
    n_blocks = n_assign // bm + N_EXPERTS
    padded = (counts + bm - 1) // bm * bm
    padded_end = jnp.cumsum(padded)
    padded_start = padded_end - padded
    dest = (padded_start[idx] + rank).reshape(-1).astype(jnp.int32)
    tok = jnp.arange(n_assign, dtype=jnp.int32) // TOP_K
    row_tok = jnp.zeros((n_blocks * bm,), jnp.int32).at[dest].set(tok)
    block_expert = jnp.minimum(
        jnp.searchsorted(padded_end, jnp.arange(n_blocks, dtype=jnp.int32) * bm, side="right"),
        N_EXPERTS - 1).astype(jnp.int32)
    n_used = (padded_end[-1:] // bm).astype(jnp.int32)
    return dest, row_tok, block_expert, n_used


def kernel(x, norm_mix_g, w_in, b_gate, rel_bias, w_branch_a, w_branch_b, w_out, norm_ffn_g,
           w_router, b_router, w_gate_up, b_gate_up, w_down, b_down, norm_final_g):
    batch, seq, d = x.shape
    t = batch * seq
    tl = _Tiles
    width_a = N_HEADS_A * HEAD_DIM
    width_b = N_HEADS_B * HEAD_DIM
    qkv_width = 3 * width_a + 3 * width_b
    xt = x.reshape(t, d)

    xn = _rmsnorm(xt, norm_mix_g, tm=tl.norm_rows, out_dtype=BF16)
    qkv = _mm(xn, w_in, col_off=0, n_out=qkv_width, tm=tl.mm_rows, tn=tl.mm_cols,
              out_dtype=BF16, name="proj_qkv")
    gates = _mm(xn, w_in, col_off=qkv_width, n_out=2 * d, tm=tl.mm_rows, tn=tl.mm_cols,
                out_dtype=BF16, bias=b_gate, act="sigmoid", name="proj_gates")
    att_a = _chunk_attention(qkv, _chunk_bias_table(rel_bias, tl.chunk_q),
                             batch=batch, seq=seq, tq=tl.chunk_q)
    att_b = _stick_attention(qkv, batch=batch, seq=seq, tq=tl.stick_q, tk=tl.stick_k,
                             col_off=3 * width_a)
    merged = _branch_merge(att_a, att_b, w_branch_a, w_branch_b, gates, tm=tl.mm_rows, tn=tl.mm_cols)
    h = _mm(merged, w_out, col_off=0, n_out=d, tm=tl.mm_rows, tn=tl.mm_cols,
            out_dtype=F32, res=xt, name="out_proj")

    xn2, idx_l, wt_l, rank_l, cnt = _router(h, norm_ffn_g, w_router, b_router, tm=tl.router_rows)
    counts = cnt[0, :N_EXPERTS].astype(jnp.int32)
    dest, row_tok, block_expert, n_used = _moe_layout(
        idx_l[:, :TOP_K], rank_l[:, :TOP_K], counts, bm=tl.moe_block)
    xs = _gather_rows(row_tok, xn2, bm=tl.moe_block)
    hidden = _expert_up(block_expert, n_used, xs, w_gate_up, b_gate_up, bm=tl.moe_block, tf=tl.up_cols)
    ys = _expert_down(block_expert, n_used, hidden, w_down, b_down, bm=tl.moe_block, tn=tl.down_cols)
    y = _combine(dest, h, wt_l, ys, norm_final_g, tm=tl.combine_rows)
    return y.reshape(batch, seq, d)
```

```python
import functools
import math

import jax
import jax.numpy as jnp
import numpy as np
from jax import lax
from jax.experimental import pallas as pl
from jax.experimental.pallas import tpu as pltpu

F32 = jnp.float32
BF16 = jnp.bfloat16

HEAD_DIM = 128
N_HEADS_A = 16
N_HEADS_B = 16
CHUNK = 64
LEFT_CHUNKS = 8
REL_CLIP = 256
N_EXPERTS = 32
TOP_K = 4
SWIGLU_ALPHA = 1.702
SWIGLU_LIMIT = 7.0
EPS = 1e-5
NEG = -1e30

V7X_LANES = 128
V7X_VMEM_BYTES = 64 * 1024 * 1024
VMEM_LIMIT = V7X_VMEM_BYTES - 8 * 1024 * 1024


def _cparams(n_axes):
    return pltpu.CompilerParams(
        dimension_semantics=("arbitrary",) * n_axes, vmem_limit_bytes=VMEM_LIMIT)


def _rmsnorm_kernel(x_ref, g_ref, o_ref):
    x = x_ref[...]
    ms = jnp.mean(x * x, axis=-1, keepdims=True)
    o_ref[...] = (x * lax.rsqrt(ms + EPS) * g_ref[...]).astype(o_ref.dtype)


def _rmsnorm(x, g, *, tm, out_dtype):
    t, d = x.shape
    return pl.pallas_call(
        _rmsnorm_kernel,
        grid=(t // tm,),
        in_specs=[pl.BlockSpec((tm, d), lambda i: (i, 0)),
                  pl.BlockSpec((1, d), lambda i: (0, 0))],
        out_specs=pl.BlockSpec((tm, d), lambda i: (i, 0)),
        out_shape=jax.ShapeDtypeStruct((t, d), out_dtype),
        compiler_params=_cparams(1),
        name="rmsnorm",
    )(x, g.reshape(1, d))


def _mm_kernel(*refs, has_bias, has_res, act):
    a_ref, w_ref = refs[0], refs[1]
    pos = 2
    b_ref = r_ref = None
    if has_bias:
        b_ref = refs[pos]
        pos += 1
    if has_res:
        r_ref = refs[pos]
        pos += 1
    o_ref = refs[pos]

    acc = jnp.dot(a_ref[...], w_ref[...].astype(BF16), preferred_element_type=F32)
    if has_bias:
        acc = acc + b_ref[...]
    if act == "sigmoid":
        acc = jax.nn.sigmoid(acc)
    if has_res:
        acc = acc + r_ref[...]
    o_ref[...] = acc.astype(o_ref.dtype)


def _mm(a, w, *, col_off, n_out, tm, tn, out_dtype, bias=None, res=None, act=None, name):
    m, k = a.shape
    assert w.shape[0] == k and col_off % tn == 0 and n_out % tn == 0 and m % tm == 0
    off = col_off // tn
    in_specs = [pl.BlockSpec((tm, k), lambda j, i: (i, 0)),
                pl.BlockSpec((k, tn), lambda j, i: (0, j + off))]
    args = [a, w]
    if bias is not None:
        in_specs.append(pl.BlockSpec((1, tn), lambda j, i: (0, j)))
        args.append(bias.reshape(1, n_out))
    if res is not None:
        in_specs.append(pl.BlockSpec((tm, tn), lambda j, i: (i, j)))
        args.append(res)
    kern = functools.partial(_mm_kernel, has_bias=bias is not None, has_res=res is not None, act=act)
    return pl.pallas_call(
        kern,
        grid=(n_out // tn, m // tm),
        in_specs=in_specs,
        out_specs=pl.BlockSpec((tm, tn), lambda j, i: (i, j)),
        out_shape=jax.ShapeDtypeStruct((m, n_out), out_dtype),
        compiler_params=_cparams(2),
        name=name,
    )(*args)


def _branch_kernel(a_ref, b_ref, wa_ref, wb_ref, ga_ref, gb_ref, o_ref):
    ya = jnp.dot(a_ref[...], wa_ref[...].astype(BF16), preferred_element_type=F32)
    yb = jnp.dot(b_ref[...], wb_ref[...].astype(BF16), preferred_element_type=F32)
    o_ref[...] = (ga_ref[...].astype(F32) * ya + gb_ref[...].astype(F32) * yb).astype(o_ref.dtype)


def _branch_merge(att_a, att_b, wa, wb, gates, *, tm, tn):
    m, ka = att_a.shape
    kb = att_b.shape[1]
    d = wa.shape[1]
    nj = d // tn
    return pl.pallas_call(
        _branch_kernel,
        grid=(nj, m // tm),
        in_specs=[pl.BlockSpec((tm, ka), lambda j, i: (i, 0)),
                  pl.BlockSpec((tm, kb), lambda j, i: (i, 0)),
                  pl.BlockSpec((ka, tn), lambda j, i: (0, j)),
                  pl.BlockSpec((kb, tn), lambda j, i: (0, j)),
                  pl.BlockSpec((tm, tn), lambda j, i: (i, j)),
                  pl.BlockSpec((tm, tn), lambda j, i: (i, j + nj))],
        out_specs=pl.BlockSpec((tm, tn), lambda j, i: (i, j)),
        out_shape=jax.ShapeDtypeStruct((m, d), BF16),
        compiler_params=_cparams(2),
        name="branch_merge",
    )(att_a, att_b, wa, wb, gates, gates)


def _chunk_bias_table(rel_bias, tq):
    left = LEFT_CHUNKS * CHUNK
    nk = 3 * tq
    assert left == 2 * tq
    qi = np.arange(tq)[:, None]
    kw = np.arange(nk)[None, :]
    krel = kw - left
    qc = qi // CHUNK
    kc = np.floor_divide(krel, CHUNK)
    valid = (kc >= qc - LEFT_CHUNKS) & (kc <= qc)
    rel_min, rel_max = left - nk + 1, left + tq - 1
    assert -REL_CLIP <= rel_min and rel_max > REL_CLIP
    rb = rel_bias.astype(F32).T
    nh = rb.shape[0]
    by_rel = jnp.concatenate(
        [rb[:, rel_min + REL_CLIP:], jnp.broadcast_to(rb[:, -1:], (nh, rel_max - REL_CLIP))], axis=1)
    period = rel_max - rel_min + 1
    w = jnp.concatenate([by_rel[:, :nk][:, ::-1], by_rel[:, nk:][:, ::-1]], axis=1)
    bias = jnp.tile(w, (1, tq))[:, :tq * (period - 1)].reshape(nh, tq, period - 1)[:, :, :nk]
    return jnp.where(jnp.asarray(valid)[None], bias, NEG)


def _chunk_attn_kernel(q_ref, k_ref, v_ref, bias_ref, o_ref, *, tq, heads, scale):
    i = pl.program_id(2)
    starts = [pl.multiple_of(jnp.maximum(i - 2 + j, 0) * tq, tq) for j in range(3)]
    for h in range(heads):
        hs = slice(h * HEAD_DIM, (h + 1) * HEAD_DIM)
        q = q_ref[:, hs]
        s_blocks = []
        for j in range(3):
            k = k_ref[pl.ds(starts[j], tq), hs]
            s = lax.dot_general(q, k, (((1,), (1,)), ((), ())), preferred_element_type=F32) * scale
            s = s + bias_ref[h, :, j * tq:(j + 1) * tq]
            if j < 2:
                s = jnp.where(i - 2 + j >= 0, s, NEG)
            s_blocks.append(s)
        m = jnp.maximum(jnp.maximum(jnp.max(s_blocks[0], axis=-1, keepdims=True),
                                    jnp.max(s_blocks[1], axis=-1, keepdims=True)),
                        jnp.max(s_blocks[2], axis=-1, keepdims=True))
        l = jnp.zeros_like(m)
        acc = jnp.zeros((tq, HEAD_DIM), F32)
        for j in range(3):
            p = jnp.exp(s_blocks[j] - m)
            l = l + jnp.sum(p, axis=-1, keepdims=True)
            v = v_ref[pl.ds(starts[j], tq), hs]
            acc = acc + jnp.dot(p.astype(BF16), v, preferred_element_type=F32)
        o_ref[:, hs] = (acc / l).astype(o_ref.dtype)


def _chunk_attention(qkv, bias_tab, *, batch, seq, tq, heads):
    nq = seq // tq
    nh = N_HEADS_A
    assert nh % heads == 0
    ng = nh // heads
    width = heads * HEAD_DIM
    kern = functools.partial(_chunk_attn_kernel, tq=tq, heads=heads, scale=1.0 / math.sqrt(HEAD_DIM))
    return pl.pallas_call(
        kern,
        grid=(batch, ng, nq),
        in_specs=[pl.BlockSpec((tq, width), lambda b, g, i: (b * nq + i, g)),
                  pl.BlockSpec((seq, width), lambda b, g, i: (b, ng + g)),
                  pl.BlockSpec((seq, width), lambda b, g, i: (b, 2 * ng + g)),
                  pl.BlockSpec((heads, tq, 3 * tq), lambda b, g, i: (g, 0, 0))],
        out_specs=pl.BlockSpec((tq, width), lambda b, g, i: (b * nq + i, g)),
        out_shape=jax.ShapeDtypeStruct((batch * seq, nh * HEAD_DIM), BF16),
        compiler_params=_cparams(3),
        name="chunk_attention",
    )(qkv, qkv, qkv, bias_tab)


def _suffix_matrix(tk):
    j = np.arange(tk)[:, None]
    s = np.arange(tk)[None, :]
    later = -(j > s).astype(np.float32)
    half = np.concatenate([later, -np.ones((tk, tk), np.float32)], axis=1)
    return jnp.asarray(np.concatenate([half, half], axis=0), BF16)


def _softplus(z):
    return jnp.maximum(z, 0.0) + jnp.log(1.0 + jnp.exp(-jnp.abs(z)))


def _stick_kernel(q_ref, k_ref, v_ref, u_ref, o_ref, acc_ref, run_ref, *, tq, tk, heads, scale):
    i = pl.program_id(2)
    u = u_ref[...]
    row = lax.broadcasted_iota(jnp.int32, (tq, tq), 0)
    col = lax.broadcasted_iota(jnp.int32, (tq, tq), 1)
    causal = col < row

    def span(h, start, diagonal):
        hs = slice(h * HEAD_DIM, (h + 1) * HEAD_DIM)
        q = q_ref[:, hs]
        k = k_ref[pl.ds(start, tq), hs]
        v = v_ref[pl.ds(start, tq), hs]
        z = lax.dot_general(q, k, (((1,), (1,)), ((), ())), preferred_element_type=F32) * scale
        sp = _softplus(z)
        neg_log_keep = jnp.where(causal, sp, 0.0) if diagonal else sp
        run = run_ref[h]
        later = [None, None]
        for half in (1, 0):
            s = neg_log_keep[:, half * tk:(half + 1) * tk]
            hi = s.astype(BF16)
            lo = (s - hi.astype(F32)).astype(BF16)
            sums = jnp.dot(jnp.concatenate([hi, lo], axis=1), u, preferred_element_type=F32)
            later[half] = sums[:, :tk] + run
            run = run + sums[:, tk:]
        run_ref[h] = run
        a = jnp.exp(z - sp + jnp.concatenate(later, axis=1))
        if diagonal:
            a = jnp.where(causal, a, 0.0)
        acc_ref[h] += jnp.dot(a.astype(BF16), v, preferred_element_type=F32)

    acc_ref[...] = jnp.zeros_like(acc_ref)
    run_ref[...] = jnp.zeros_like(run_ref)
    for h in range(heads):
        span(h, pl.multiple_of(i * tq, tq), True)

    def body(t, c):
        start = pl.multiple_of((i - 1 - t) * tq, tq)
        for h in range(heads):
            span(h, start, False)
        return c

    lax.fori_loop(0, i, body, 0)
    for h in range(heads):
        o_ref[:, h * HEAD_DIM:(h + 1) * HEAD_DIM] = acc_ref[h].astype(o_ref.dtype)


def _stick_attention(qkv, *, batch, seq, tq, heads, col_off):
    nq = seq // tq
    nh = N_HEADS_B
    tk = tq // 2
    width = heads * HEAD_DIM
    assert col_off % width == 0 and nh % heads == 0
    c0 = col_off // width
    ng = nh // heads
    kern = functools.partial(_stick_kernel, tq=tq, tk=tk, heads=heads, scale=1.0 / math.sqrt(HEAD_DIM))
    return pl.pallas_call(
        kern,
        grid=(batch, ng, nq),
        in_specs=[pl.BlockSpec((tq, width), lambda b, g, i: (b * nq + i, c0 + g)),
                  pl.BlockSpec((seq, width), lambda b, g, i: (b, c0 + ng + g)),
                  pl.BlockSpec((seq, width), lambda b, g, i: (b, c0 + 2 * ng + g)),
                  pl.BlockSpec((2 * tk, 2 * tk), lambda b, g, i: (0, 0))],
        out_specs=pl.BlockSpec((tq, width), lambda b, g, i: (b * nq + i, g)),
        out_shape=jax.ShapeDtypeStruct((batch * seq, nh * HEAD_DIM), BF16),
        scratch_shapes=[pltpu.VMEM((heads, tq, HEAD_DIM), F32), pltpu.VMEM((heads, tq, tk), F32)],
        compiler_params=_cparams(3),
        name="stick_attention",
    )(qkv, qkv, qkv, _suffix_matrix(tk))


def _router_kernel(h_ref, g_ref, wr_ref, br_ref, xn_ref, idx_ref, wt_ref, rank_ref, cnt_ref,
                   wr_hi, wr_lo, carry, *, tm):
    step = pl.program_id(0)

    @pl.when(step == 0)
    def _():
        w = wr_ref[...]
        hi = w.astype(BF16)
        wr_hi[...] = hi
        wr_lo[...] = (w - hi.astype(F32)).astype(BF16)
        carry[...] = jnp.zeros_like(carry)

    x = h_ref[...]
    ms = jnp.mean(x * x, axis=-1, keepdims=True)
    xn = x * lax.rsqrt(ms + EPS) * g_ref[...]
    xn_ref[...] = xn
    xh = xn.astype(BF16)
    xl = (xn - xh.astype(F32)).astype(BF16)
    logits = (jnp.dot(xh, wr_hi[...], preferred_element_type=F32)
              + jnp.dot(xh, wr_lo[...], preferred_element_type=F32)
              + jnp.dot(xl, wr_hi[...], preferred_element_type=F32)) + br_ref[...]
    lane = lax.broadcasted_iota(jnp.int32, (tm, V7X_LANES), 1)
    logits = jnp.where(lane < N_EXPERTS, logits, -jnp.inf)

    r = lax.broadcasted_iota(jnp.int32, (tm, tm), 0)
    c = lax.broadcasted_iota(jnp.int32, (tm, tm), 1)
    tri = (c < r).astype(BF16)

    sel_mask = jnp.zeros((tm, V7X_LANES), F32)
    onehots, vals = [], []
    work = logits
    for _ in range(TOP_K):
        mx = jnp.max(work, axis=-1, keepdims=True)
        first = jnp.min(jnp.where(work == mx, lane, V7X_LANES), axis=-1, keepdims=True)
        oh = lane == first
        onehots.append(oh)
        vals.append(mx)
        sel_mask = sel_mask + oh.astype(F32)
        work = jnp.where(oh, -jnp.inf, work)

    prefix = jnp.dot(tri, sel_mask.astype(BF16), preferred_element_type=F32) + carry[...]
    carry[...] = carry[...] + jnp.sum(sel_mask, axis=0, keepdims=True)
    cnt_ref[...] = jnp.broadcast_to(carry[...], cnt_ref.shape)

    es = [jnp.exp(v - vals[0]) for v in vals]
    denom = es[0] + es[1] + es[2] + es[3]
    idx_out = jnp.zeros((tm, V7X_LANES), jnp.int32)
    wt_out = jnp.zeros((tm, V7X_LANES), F32)
    rank_out = jnp.zeros((tm, V7X_LANES), jnp.int32)
    for k in range(TOP_K):
        oh = onehots[k]
        e_k = jnp.sum(jnp.where(oh, lane, 0), axis=-1, keepdims=True)
        r_k = jnp.sum(jnp.where(oh, prefix, 0.0), axis=-1, keepdims=True).astype(jnp.int32)
        idx_out = jnp.where(lane == k, e_k, idx_out)
        wt_out = jnp.where(lane == k, es[k] / denom, wt_out)
        rank_out = jnp.where(lane == k, r_k, rank_out)
    idx_ref[...] = idx_out
    wt_ref[...] = wt_out
    rank_ref[...] = rank_out


def _router(h, g, w_router, b_router, *, tm):
    t, d = h.shape
    wr = jnp.zeros((d, V7X_LANES), F32).at[:, :N_EXPERTS].set(w_router)
    br = jnp.zeros((1, V7X_LANES), F32).at[0, :N_EXPERTS].set(b_router)
    row = lambda i: (i, 0)
    fixed = lambda i: (0, 0)
    kern = functools.partial(_router_kernel, tm=tm)
    return pl.pallas_call(
        kern,
        grid=(t // tm,),
        in_specs=[pl.BlockSpec((tm, d), row),
                  pl.BlockSpec((1, d), fixed),
                  pl.BlockSpec((d, V7X_LANES), fixed),
                  pl.BlockSpec((1, V7X_LANES), fixed)],
        out_specs=[pl.BlockSpec((tm, d), row),
                   pl.BlockSpec((tm, V7X_LANES), row),
                   pl.BlockSpec((tm, V7X_LANES), row),
                   pl.BlockSpec((tm, V7X_LANES), row),
                   pl.BlockSpec((8, V7X_LANES), fixed)],
        out_shape=[jax.ShapeDtypeStruct((t, d), F32),
                   jax.ShapeDtypeStruct((t, V7X_LANES), jnp.int32),
                   jax.ShapeDtypeStruct((t, V7X_LANES), F32),
                   jax.ShapeDtypeStruct((t, V7X_LANES), jnp.int32),
                   jax.ShapeDtypeStruct((8, V7X_LANES), F32)],
        scratch_shapes=[pltpu.VMEM((d, V7X_LANES), BF16),
                        pltpu.VMEM((d, V7X_LANES), BF16),
                        pltpu.VMEM((1, V7X_LANES), F32)],
        compiler_params=_cparams(1),
        name="router",
    )(h, g.reshape(1, d), wr, br)


def _row_copy(src_hbm, dst, sem, src_row, dst_row):
    return pltpu.make_async_copy(src_hbm.at[pl.ds(src_row, 1)], dst.at[pl.ds(dst_row, 1)], sem)


def _gather_kernel(tok_ref, nu_ref, x_hbm, o_ref, buf, sem, *, bm):
    i = pl.program_id(0)
    n_used = nu_ref[0]

    def issue(block):
        slot = block % 2

        def body(r, c):
            _row_copy(x_hbm, buf.at[slot], sem.at[slot], tok_ref[block * bm + r], r).start()
            return c

        lax.fori_loop(0, bm, body, 0, unroll=8)

    @pl.when(i == 0)
    def _():
        issue(i)

    @pl.when(i + 1 < n_used)
    def _():
        issue(i + 1)

    @pl.when(i < n_used)
    def _():
        slot = i % 2
        pltpu.make_async_copy(x_hbm.at[pl.ds(0, bm)], buf.at[slot], sem.at[slot]).wait()
        o_ref[...] = buf[slot].astype(o_ref.dtype)

    @pl.when(i >= n_used)
    def _():
        o_ref[...] = jnp.zeros_like(o_ref)


def _gather_rows(row_tok, n_used, xn, *, bm):
    n_rows = row_tok.shape[0]
    d = xn.shape[1]
    kern = functools.partial(_gather_kernel, bm=bm)
    return pl.pallas_call(
        kern,
        grid_spec=pltpu.PrefetchScalarGridSpec(
            num_scalar_prefetch=2,
            grid=(n_rows // bm,),
            in_specs=[pl.BlockSpec(memory_space=pl.ANY)],
            out_specs=pl.BlockSpec((bm, d), lambda i, tok, nu: (i, 0)),
            scratch_shapes=[pltpu.VMEM((2, bm, d), F32), pltpu.SemaphoreType.DMA((2,))]),
        out_shape=jax.ShapeDtypeStruct((n_rows, d), BF16),
        compiler_params=_cparams(1),
        name="gather_rows",
    )(row_tok, n_used, xn)


def _expert_up_kernel(be_ref, nu_ref, xs_ref, wg_ref, wl_ref, bg_ref, bl_ref, h_ref):
    i = pl.program_id(1)

    @pl.when(i < nu_ref[0])
    def _():
        x = xs_ref[...]
        g = jnp.dot(x, wg_ref[0].astype(BF16), preferred_element_type=F32) + bg_ref[0]
        l = jnp.dot(x, wl_ref[0].astype(BF16), preferred_element_type=F32) + bl_ref[0]
        glu = jnp.minimum(g, SWIGLU_LIMIT)
        lin = jnp.clip(l, -SWIGLU_LIMIT, SWIGLU_LIMIT)
        h_ref[...] = (glu * jax.nn.sigmoid(SWIGLU_ALPHA * glu) * (lin + 1.0)).astype(h_ref.dtype)

    @pl.when(i >= nu_ref[0])
    def _():
        h_ref[...] = jnp.zeros_like(h_ref)


def _expert_up(block_expert, n_used, xs, w_gate_up, b_gate_up, *, bm, tf):
    n_rows, d = xs.shape
    n_e, _, two_ff = w_gate_up.shape
    d_ff = two_ff // 2
    nj = d_ff // tf

    def blk(i, nu):
        return jnp.minimum(i, nu[0] - 1)

    return pl.pallas_call(
        _expert_up_kernel,
        grid_spec=pltpu.PrefetchScalarGridSpec(
            num_scalar_prefetch=2,
            grid=(nj, n_rows // bm),
            in_specs=[pl.BlockSpec((bm, d), lambda j, i, be, nu: (blk(i, nu), 0)),
                      pl.BlockSpec((1, d, tf), lambda j, i, be, nu: (be[blk(i, nu)], 0, j)),
                      pl.BlockSpec((1, d, tf), lambda j, i, be, nu: (be[blk(i, nu)], 0, nj + j)),
                      pl.BlockSpec((1, 1, tf), lambda j, i, be, nu: (be[blk(i, nu)], 0, j)),
                      pl.BlockSpec((1, 1, tf), lambda j, i, be, nu: (be[blk(i, nu)], 0, nj + j))],
            out_specs=pl.BlockSpec((bm, tf), lambda j, i, be, nu: (i, j))),
        out_shape=jax.ShapeDtypeStruct((n_rows, d_ff), BF16),
        compiler_params=_cparams(2),
        name="expert_up",
    )(block_expert, n_used, xs, w_gate_up, w_gate_up,
      b_gate_up.reshape(n_e, 1, two_ff), b_gate_up.reshape(n_e, 1, two_ff))


def _expert_down_kernel(be_ref, nu_ref, h_ref, w_ref, b_ref, y_ref):
    i = pl.program_id(1)

    @pl.when(i < nu_ref[0])
    def _():
        y_ref[...] = (jnp.dot(h_ref[...], w_ref[0].astype(BF16), preferred_element_type=F32)
                      + b_ref[0]).astype(y_ref.dtype)

    @pl.when(i >= nu_ref[0])
    def _():
        y_ref[...] = jnp.zeros_like(y_ref)


def _expert_down(block_expert, n_used, hidden, w_down, b_down, *, bm, tn):
    n_rows, d_ff = hidden.shape
    n_e, _, d = w_down.shape

    def blk(i, nu):
        return jnp.minimum(i, nu[0] - 1)

    return pl.pallas_call(
        _expert_down_kernel,
        grid_spec=pltpu.PrefetchScalarGridSpec(
            num_scalar_prefetch=2,
            grid=(d // tn, n_rows // bm),
            in_specs=[pl.BlockSpec((bm, d_ff), lambda j, i, be, nu: (blk(i, nu), 0)),
                      pl.BlockSpec((1, d_ff, tn), lambda j, i, be, nu: (be[blk(i, nu)], 0, j)),
                      pl.BlockSpec((1, 1, tn), lambda j, i, be, nu: (be[blk(i, nu)], 0, j))],
            out_specs=pl.BlockSpec((bm, tn), lambda j, i, be, nu: (i, j))),
        out_shape=jax.ShapeDtypeStruct((n_rows, d), F32),
        compiler_params=_cparams(2),
        name="expert_down",
    )(block_expert, n_used, hidden, w_down, b_down.reshape(n_e, 1, d))


def _combine_kernel(dest_ref, h_ref, wt_ref, ys_hbm, g_ref, o_ref, buf, sem, *, tm):
    i = pl.program_id(0)

    def issue(tile):
        slot = tile % 2

        def body(r, c):
            for k in range(TOP_K):
                _row_copy(ys_hbm, buf.at[slot, k], sem.at[slot],
                          dest_ref[(tile * tm + r) * TOP_K + k], r).start()
            return c

        lax.fori_loop(0, tm, body, 0, unroll=4)

    @pl.when(i == 0)
    def _():
        issue(i)

    @pl.when(i + 1 < pl.num_programs(0))
    def _():
        issue(i + 1)

    slot = i % 2
    for k in range(TOP_K):
        pltpu.make_async_copy(ys_hbm.at[pl.ds(0, tm)], buf.at[slot, k], sem.at[slot]).wait()
    wt = wt_ref[...]
    y = h_ref[...]
    for k in range(TOP_K):
        y = y + wt[:, k:k + 1] * buf[slot, k]
    ms = jnp.mean(y * y, axis=-1, keepdims=True)
    o_ref[...] = y * lax.rsqrt(ms + EPS) * g_ref[...]


def _combine(dest_flat, h, wts, ys, g, *, tm):
    t, d = h.shape
    kern = functools.partial(_combine_kernel, tm=tm)
    return pl.pallas_call(
        kern,
        grid_spec=pltpu.PrefetchScalarGridSpec(
            num_scalar_prefetch=1,
            grid=(t // tm,),
            in_specs=[pl.BlockSpec((tm, d), lambda i, dst: (i, 0)),
                      pl.BlockSpec((tm, V7X_LANES), lambda i, dst: (i, 0)),
                      pl.BlockSpec(memory_space=pl.ANY),
                      pl.BlockSpec((1, d), lambda i, dst: (0, 0))],
            out_specs=pl.BlockSpec((tm, d), lambda i, dst: (i, 0)),
            scratch_shapes=[pltpu.VMEM((2, TOP_K, tm, d), F32), pltpu.SemaphoreType.DMA((2,))]),
        out_shape=jax.ShapeDtypeStruct((t, d), F32),
        compiler_params=_cparams(1),
        name="combine",
    )(dest_flat, h, wts, ys, g.reshape(1, d))


class _Tiles:
    norm_rows = 256
    mm_rows = 1024
    mm_cols = 512
    chunk_q = 256
    chunk_heads = 2
    stick_q = 256
    stick_heads = 4
    router_rows = 256
    moe_block = 256
    up_cols = 512
    down_cols = 2048
    combine_rows = 128


def _moe_layout(idx, rank, counts, *, bm):
    t = idx.shape[0]
    n_assign = t * TOP_K
    n_blocks = n_assign // bm + N_EXPERTS
    padded = (counts + bm - 1) // bm * bm
    padded_end = jnp.cumsum(padded)
    padded_start = padded_end - padded
    dest = (padded_start[idx] + rank).reshape(-1).astype(jnp.int32)
    tok = jnp.arange(n_assign, dtype=jnp.int32) // TOP_K
    row_tok = jnp.zeros((n_blocks * bm,), jnp.int32).at[dest].set(tok)
    blk_start = jnp.arange(n_blocks, dtype=jnp.int32) * bm
    block_expert = jnp.minimum(
        jnp.sum((padded_end[None, :] <= blk_start[:, None]).astype(jnp.int32), axis=1), N_EXPERTS - 1)
    n_used = (padded_end[-1:] // bm).astype(jnp.int32)
    return dest, row_tok, block_expert, n_used


def kernel(x, norm_mix_g, w_in, b_gate, rel_bias, w_branch_a, w_branch_b, w_out, norm_ffn_g,
           w_router, b_router, w_gate_up, b_gate_up, w_down, b_down, norm_final_g):
    batch, seq, d = x.shape
    t = batch * seq
    tl = _Tiles
    width_a = N_HEADS_A * HEAD_DIM
    width_b = N_HEADS_B * HEAD_DIM
    qkv_width = 3 * width_a + 3 * width_b
    xt = x.reshape(t, d)

    xn = _rmsnorm(xt, norm_mix_g, tm=tl.norm_rows, out_dtype=BF16)
    qkv = _mm(xn, w_in, col_off=0, n_out=qkv_width, tm=tl.mm_rows, tn=tl.mm_cols,
              out_dtype=BF16, name="proj_qkv")
    gates = _mm(xn, w_in, col_off=qkv_width, n_out=2 * d, tm=tl.mm_rows, tn=tl.mm_cols,
                out_dtype=BF16, bias=b_gate, act="sigmoid", name="proj_gates")
    att_a = _chunk_attention(qkv, _chunk_bias_table(rel_bias, tl.chunk_q),
                             batch=batch, seq=seq, tq=tl.chunk_q, heads=tl.chunk_heads)
    att_b = _stick_attention(qkv, batch=batch, seq=seq, tq=tl.stick_q, heads=tl.stick_heads,
                             col_off=3 * width_a)
    merged = _branch_merge(att_a, att_b, w_branch_a, w_branch_b, gates, tm=tl.mm_rows, tn=tl.mm_cols)
    h = _mm(merged, w_out, col_off=0, n_out=d, tm=tl.mm_rows, tn=tl.mm_cols,
            out_dtype=F32, res=xt, name="out_proj")

    xn2, idx_l, wt_l, rank_l, cnt = _router(h, norm_ffn_g, w_router, b_router, tm=tl.router_rows)
    counts = cnt[0, :N_EXPERTS].astype(jnp.int32)
    dest, row_tok, block_expert, n_used = _moe_layout(
        idx_l[:, :TOP_K], rank_l[:, :TOP_K], counts, bm=tl.moe_block)
    xs = _gather_rows(row_tok, n_used, xn2, bm=tl.moe_block)
    hidden = _expert_up(block_expert, n_used, xs, w_gate_up, b_gate_up, bm=tl.moe_block, tf=tl.up_cols)
    ys = _expert_down(block_expert, n_used, hidden, w_down, b_down, bm=tl.moe_block, tn=tl.down_cols)
    y = _combine(dest, h, wt_l, ys, norm_final_g, tm=tl.combine_rows)
    return y.reshape(batch, seq, d)
```

```python
import functools
import math

import jax
import jax.numpy as jnp
import numpy as np
from jax import lax
from jax.experimental import pallas as pl
from jax.experimental.pallas import tpu as pltpu

F32 = jnp.float32
BF16 = jnp.bfloat16

HEAD_DIM = 128
N_HEADS_A = 16
N_HEADS_B = 16
CHUNK = 64
LEFT_CHUNKS = 8
REL_CLIP = 256
N_EXPERTS = 32
TOP_K = 4
SWIGLU_ALPHA = 1.702
SWIGLU_LIMIT = 7.0
EPS = 1e-5
NEG = -1e30

V7X_LANES = 128
V7X_VMEM_BYTES = 64 * 1024 * 1024
VMEM_LIMIT = V7X_VMEM_BYTES - 8 * 1024 * 1024


def _cparams(n_axes):
    return pltpu.CompilerParams(
        dimension_semantics=("arbitrary",) * n_axes, vmem_limit_bytes=VMEM_LIMIT)


def _rmsnorm_kernel(x_ref, g_ref, o_ref):
    x = x_ref[...]
    ms = jnp.mean(x * x, axis=-1, keepdims=True)
    o_ref[...] = (x * lax.rsqrt(ms + EPS) * g_ref[...]).astype(o_ref.dtype)


def _rmsnorm(x, g, *, tm, out_dtype):
    t, d = x.shape
    return pl.pallas_call(
        _rmsnorm_kernel,
        grid=(t // tm,),
        in_specs=[pl.BlockSpec((tm, d), lambda i: (i, 0)),
                  pl.BlockSpec((1, d), lambda i: (0, 0))],
        out_specs=pl.BlockSpec((tm, d), lambda i: (i, 0)),
        out_shape=jax.ShapeDtypeStruct((t, d), out_dtype),
        compiler_params=_cparams(1),
        name="rmsnorm",
    )(x, g.reshape(1, d))


def _mm_kernel(*refs, has_bias, has_res, act):
    a_ref, w_ref = refs[0], refs[1]
    pos = 2
    b_ref = r_ref = None
    if has_bias:
        b_ref = refs[pos]
        pos += 1
    if has_res:
        r_ref = refs[pos]
        pos += 1
    o_ref = refs[pos]

    acc = jnp.dot(a_ref[...], w_ref[...].astype(BF16), preferred_element_type=F32)
    if has_bias:
        acc = acc + b_ref[...]
    if act == "sigmoid":
        acc = jax.nn.sigmoid(acc)
    if has_res:
        acc = acc + r_ref[...]
    o_ref[...] = acc.astype(o_ref.dtype)


def _mm(a, w, *, col_off, n_out, tm, tn, out_dtype, bias=None, res=None, act=None, name):
    m, k = a.shape
    assert w.shape[0] == k and col_off % tn == 0 and n_out % tn == 0 and m % tm == 0
    off = col_off // tn
    in_specs = [pl.BlockSpec((tm, k), lambda j, i: (i, 0)),
                pl.BlockSpec((k, tn), lambda j, i: (0, j + off))]
    args = [a, w]
    if bias is not None:
        in_specs.append(pl.BlockSpec((1, tn), lambda j, i: (0, j)))
        args.append(bias.reshape(1, n_out))
    if res is not None:
        in_specs.append(pl.BlockSpec((tm, tn), lambda j, i: (i, j)))
        args.append(res)
    kern = functools.partial(_mm_kernel, has_bias=bias is not None, has_res=res is not None, act=act)
    return pl.pallas_call(
        kern,
        grid=(n_out // tn, m // tm),
        in_specs=in_specs,
        out_specs=pl.BlockSpec((tm, tn), lambda j, i: (i, j)),
        out_shape=jax.ShapeDtypeStruct((m, n_out), out_dtype),
        compiler_params=_cparams(2),
        name=name,
    )(*args)


def _branch_kernel(a_ref, b_ref, wa_ref, wb_ref, ga_ref, gb_ref, o_ref):
    ya = jnp.dot(a_ref[...], wa_ref[...].astype(BF16), preferred_element_type=F32)
    yb = jnp.dot(b_ref[...], wb_ref[...].astype(BF16), preferred_element_type=F32)
    o_ref[...] = (ga_ref[...].astype(F32) * ya + gb_ref[...].astype(F32) * yb).astype(o_ref.dtype)


def _branch_merge(att_a, att_b, wa, wb, gates, *, tm, tn):
    m, ka = att_a.shape
    kb = att_b.shape[1]
    d = wa.shape[1]
    nj = d // tn
    return pl.pallas_call(
        _branch_kernel,
        grid=(nj, m // tm),
        in_specs=[pl.BlockSpec((tm, ka), lambda j, i: (i, 0)),
                  pl.BlockSpec((tm, kb), lambda j, i: (i, 0)),
                  pl.BlockSpec((ka, tn), lambda j, i: (0, j)),
                  pl.BlockSpec((kb, tn), lambda j, i: (0, j)),
                  pl.BlockSpec((tm, tn), lambda j, i: (i, j)),
                  pl.BlockSpec((tm, tn), lambda j, i: (i, j + nj))],
        out_specs=pl.BlockSpec((tm, tn), lambda j, i: (i, j)),
        out_shape=jax.ShapeDtypeStruct((m, d), BF16),
        compiler_params=_cparams(2),
        name="branch_merge",
    )(att_a, att_b, wa, wb, gates, gates)


def _chunk_bias_table(rel_bias, tq):
    left = LEFT_CHUNKS * CHUNK
    nk = 3 * tq
    assert left == 2 * tq
    qi = np.arange(tq)[:, None]
    kw = np.arange(nk)[None, :]
    krel = kw - left
    qc = qi // CHUNK
    kc = np.floor_divide(krel, CHUNK)
    valid = (kc >= qc - LEFT_CHUNKS) & (kc <= qc)
    rel_min, rel_max = left - nk + 1, left + tq - 1
    assert -REL_CLIP <= rel_min and rel_max > REL_CLIP
    rb = rel_bias.astype(F32).T
    nh = rb.shape[0]
    by_rel = jnp.concatenate(
        [rb[:, rel_min + REL_CLIP:], jnp.broadcast_to(rb[:, -1:], (nh, rel_max - REL_CLIP))], axis=1)
    period = rel_max - rel_min + 1
    w = jnp.concatenate([by_rel[:, :nk][:, ::-1], by_rel[:, nk:][:, ::-1]], axis=1)
    bias = jnp.tile(w, (1, tq))[:, :tq * (period - 1)].reshape(nh, tq, period - 1)[:, :, :nk]
    return jnp.where(jnp.asarray(valid)[None], bias, NEG)


def _chunk_attn_kernel(q_ref, k_ref, v_ref, bias_ref, o_ref, *, tq, heads, scale):
    i = pl.program_id(2)
    starts = [pl.multiple_of(jnp.maximum(i - 2 + j, 0) * tq, tq) for j in range(3)]
    for h in range(heads):
        hs = slice(h * HEAD_DIM, (h + 1) * HEAD_DIM)
        q = q_ref[:, hs]
        s_blocks = []
        for j in range(3):
            k = k_ref[pl.ds(starts[j], tq), hs]
            s = lax.dot_general(q, k, (((1,), (1,)), ((), ())), preferred_element_type=F32) * scale
            s = s + bias_ref[h, :, j * tq:(j + 1) * tq]
            if j < 2:
                s = jnp.where(i - 2 + j >= 0, s, NEG)
            s_blocks.append(s)
        m = jnp.maximum(jnp.maximum(jnp.max(s_blocks[0], axis=-1, keepdims=True),
                                    jnp.max(s_blocks[1], axis=-1, keepdims=True)),
                        jnp.max(s_blocks[2], axis=-1, keepdims=True))
        l = jnp.zeros_like(m)
        acc = jnp.zeros((tq, HEAD_DIM), F32)
        for j in range(3):
            p = jnp.exp(s_blocks[j] - m)
            l = l + jnp.sum(p, axis=-1, keepdims=True)
            v = v_ref[pl.ds(starts[j], tq), hs]
            acc = acc + jnp.dot(p.astype(BF16), v, preferred_element_type=F32)
        o_ref[:, hs] = (acc / l).astype(o_ref.dtype)


def _chunk_attention(qkv, bias_tab, *, batch, seq, tq, heads):
    nq = seq // tq
    nh = N_HEADS_A
    assert nh % heads == 0
    ng = nh // heads
    width = heads * HEAD_DIM
    kern = functools.partial(_chunk_attn_kernel, tq=tq, heads=heads, scale=1.0 / math.sqrt(HEAD_DIM))
    return pl.pallas_call(
        kern,
        grid=(batch, ng, nq),
        in_specs=[pl.BlockSpec((tq, width), lambda b, g, i: (b * nq + i, g)),
                  pl.BlockSpec((seq, width), lambda b, g, i: (b, ng + g)),
                  pl.BlockSpec((seq, width), lambda b, g, i: (b, 2 * ng + g)),
                  pl.BlockSpec((heads, tq, 3 * tq), lambda b, g, i: (g, 0, 0))],
        out_specs=pl.BlockSpec((tq, width), lambda b, g, i: (b * nq + i, g)),
        out_shape=jax.ShapeDtypeStruct((batch * seq, nh * HEAD_DIM), BF16),
        compiler_params=_cparams(3),
        name="chunk_attention",
    )(qkv, qkv, qkv, bias_tab)


def _suffix_matrix(tk):
    j = np.arange(tk)[:, None]
    s = np.arange(tk)[None, :]
    later = -(j > s).astype(np.float32)
    half = np.concatenate([later, -np.ones((tk, tk), np.float32)], axis=1)
    return jnp.asarray(np.concatenate([half, half], axis=0), BF16)


def _softplus(z):
    return jnp.maximum(z, 0.0) + jnp.log(1.0 + jnp.exp(-jnp.abs(z)))


def _stick_kernel(q_ref, k_ref, v_ref, u_ref, o_ref, acc_ref, run_ref, *, tq, tk, heads, scale):
    i = pl.program_id(2)
    u = u_ref[...]
    row = lax.broadcasted_iota(jnp.int32, (tq, tq), 0)
    col = lax.broadcasted_iota(jnp.int32, (tq, tq), 1)
    causal = col < row

    def span(h, start, diagonal):
        hs = slice(h * HEAD_DIM, (h + 1) * HEAD_DIM)
        q = q_ref[:, hs]
        k = k_ref[pl.ds(start, tq), hs]
        v = v_ref[pl.ds(start, tq), hs]
        z = lax.dot_general(q, k, (((1,), (1,)), ((), ())), preferred_element_type=F32) * scale
        sp = _softplus(z)
        neg_log_keep = jnp.where(causal, sp, 0.0) if diagonal else sp
        run = run_ref[h]
        later = [None, None]
        for half in (1, 0):
            s = neg_log_keep[:, half * tk:(half + 1) * tk]
            hi = s.astype(BF16)
            lo = (s - hi.astype(F32)).astype(BF16)
            sums = jnp.dot(jnp.concatenate([hi, lo], axis=1), u, preferred_element_type=F32)
            later[half] = sums[:, :tk] + run
            run = run + sums[:, tk:]
        run_ref[h] = run
        a = jnp.exp(z - sp + jnp.concatenate(later, axis=1))
        if diagonal:
            a = jnp.where(causal, a, 0.0)
        acc_ref[h] += jnp.dot(a.astype(BF16), v, preferred_element_type=F32)

    acc_ref[...] = jnp.zeros_like(acc_ref)
    run_ref[...] = jnp.zeros_like(run_ref)
    for h in range(heads):
        span(h, pl.multiple_of(i * tq, tq), True)

    def body(t, c):
        start = pl.multiple_of((i - 1 - t) * tq, tq)
        for h in range(heads):
            span(h, start, False)
        return c

    lax.fori_loop(0, i, body, 0)
    for h in range(heads):
        o_ref[:, h * HEAD_DIM:(h + 1) * HEAD_DIM] = acc_ref[h].astype(o_ref.dtype)


def _stick_attention(qkv, *, batch, seq, tq, heads, col_off):
    nq = seq // tq
    nh = N_HEADS_B
    tk = tq // 2
    width = heads * HEAD_DIM
    assert col_off % width == 0 and nh % heads == 0
    c0 = col_off // width
    ng = nh // heads
    kern = functools.partial(_stick_kernel, tq=tq, tk=tk, heads=heads, scale=1.0 / math.sqrt(HEAD_DIM))
    return pl.pallas_call(
        kern,
        grid=(batch, ng, nq),
        in_specs=[pl.BlockSpec((tq, width), lambda b, g, i: (b * nq + i, c0 + g)),
                  pl.BlockSpec((seq, width), lambda b, g, i: (b, c0 + ng + g)),
                  pl.BlockSpec((seq, width), lambda b, g, i: (b, c0 + 2 * ng + g)),
                  pl.BlockSpec((2 * tk, 2 * tk), lambda b, g, i: (0, 0))],
        out_specs=pl.BlockSpec((tq, width), lambda b, g, i: (b * nq + i, g)),
        out_shape=jax.ShapeDtypeStruct((batch * seq, nh * HEAD_DIM), BF16),
        scratch_shapes=[pltpu.VMEM((heads, tq, HEAD_DIM), F32), pltpu.VMEM((heads, tq, tk), F32)],
        compiler_params=_cparams(3),
        name="stick_attention",
    )(qkv, qkv, qkv, _suffix_matrix(tk))


def _router_kernel(h_ref, g_ref, wr_ref, br_ref, xn_ref, idx_ref, wt_ref, rank_ref, cnt_ref,
                   wr_hi, wr_lo, carry, *, tm):
    step = pl.program_id(0)

    @pl.when(step == 0)
    def _():
        w = wr_ref[...]
        hi = w.astype(BF16)
        wr_hi[...] = hi
        wr_lo[...] = (w - hi.astype(F32)).astype(BF16)
        carry[...] = jnp.zeros_like(carry)

    x = h_ref[...]
    ms = jnp.mean(x * x, axis=-1, keepdims=True)
    xn = x * lax.rsqrt(ms + EPS) * g_ref[...]
    xn_ref[...] = xn
    xh = xn.astype(BF16)
    xl = (xn - xh.astype(F32)).astype(BF16)
    logits = (jnp.dot(xh, wr_hi[...], preferred_element_type=F32)
              + jnp.dot(xh, wr_lo[...], preferred_element_type=F32)
              + jnp.dot(xl, wr_hi[...], preferred_element_type=F32)) + br_ref[...]
    lane = lax.broadcasted_iota(jnp.int32, (tm, V7X_LANES), 1)
    logits = jnp.where(lane < N_EXPERTS, logits, -jnp.inf)

    r = lax.broadcasted_iota(jnp.int32, (tm, tm), 0)
    c = lax.broadcasted_iota(jnp.int32, (tm, tm), 1)
    tri = (c < r).astype(BF16)

    sel_mask = jnp.zeros((tm, V7X_LANES), F32)
    onehots, vals = [], []
    work = logits
    for _ in range(TOP_K):
        mx = jnp.max(work, axis=-1, keepdims=True)
        first = jnp.min(jnp.where(work == mx, lane, V7X_LANES), axis=-1, keepdims=True)
        oh = lane == first
        onehots.append(oh)
        vals.append(mx)
        sel_mask = sel_mask + oh.astype(F32)
        work = jnp.where(oh, -jnp.inf, work)

    prefix = jnp.dot(tri, sel_mask.astype(BF16), preferred_element_type=F32) + carry[...]
    carry[...] = carry[...] + jnp.sum(sel_mask, axis=0, keepdims=True)
    cnt_ref[...] = jnp.broadcast_to(carry[...], cnt_ref.shape)

    es = [jnp.exp(v - vals[0]) for v in vals]
    denom = es[0] + es[1] + es[2] + es[3]
    idx_out = jnp.zeros((tm, V7X_LANES), jnp.int32)
    wt_out = jnp.zeros((tm, V7X_LANES), F32)
    rank_out = jnp.zeros((tm, V7X_LANES), jnp.int32)
    for k in range(TOP_K):
        oh = onehots[k]
        e_k = jnp.sum(jnp.where(oh, lane, 0), axis=-1, keepdims=True)
        r_k = jnp.sum(jnp.where(oh, prefix, 0.0), axis=-1, keepdims=True).astype(jnp.int32)
        idx_out = jnp.where(lane == k, e_k, idx_out)
        wt_out = jnp.where(lane == k, es[k] / denom, wt_out)
        rank_out = jnp.where(lane == k, r_k, rank_out)
    idx_ref[...] = idx_out
    wt_ref[...] = wt_out
    rank_ref[...] = rank_out


def _router(h, g, w_router, b_router, *, tm):
    t, d = h.shape
    wr = jnp.zeros((d, V7X_LANES), F32).at[:, :N_EXPERTS].set(w_router)
    br = jnp.zeros((1, V7X_LANES), F32).at[0, :N_EXPERTS].set(b_router)
    row = lambda i: (i, 0)
    fixed = lambda i: (0, 0)
    kern = functools.partial(_router_kernel, tm=tm)
    return pl.pallas_call(
        kern,
        grid=(t // tm,),
        in_specs=[pl.BlockSpec((tm, d), row),
                  pl.BlockSpec((1, d), fixed),
                  pl.BlockSpec((d, V7X_LANES), fixed),
                  pl.BlockSpec((1, V7X_LANES), fixed)],
        out_specs=[pl.BlockSpec((tm, d), row),
                   pl.BlockSpec((tm, V7X_LANES), row),
                   pl.BlockSpec((tm, V7X_LANES), row),
                   pl.BlockSpec((tm, V7X_LANES), row),
                   pl.BlockSpec((8, V7X_LANES), fixed)],
        out_shape=[jax.ShapeDtypeStruct((t, d), F32),
                   jax.ShapeDtypeStruct((t, V7X_LANES), jnp.int32),
                   jax.ShapeDtypeStruct((t, V7X_LANES), F32),
                   jax.ShapeDtypeStruct((t, V7X_LANES), jnp.int32),
                   jax.ShapeDtypeStruct((8, V7X_LANES), F32)],
        scratch_shapes=[pltpu.VMEM((d, V7X_LANES), BF16),
                        pltpu.VMEM((d, V7X_LANES), BF16),
                        pltpu.VMEM((1, V7X_LANES), F32)],
        compiler_params=_cparams(1),
        name="router",
    )(h, g.reshape(1, d), wr, br)


def _row_copy(src_hbm, dst, sem, src_row, dst_row):
    return pltpu.make_async_copy(src_hbm.at[pl.ds(src_row, 1)], dst.at[pl.ds(dst_row, 1)], sem)


def _gather_kernel(tok_ref, nu_ref, x_hbm, o_ref, buf, sem, *, bm):
    i = pl.program_id(0)
    n_used = nu_ref[0]

    def issue(block):
        slot = block % 2

        def body(rp, c):
            for p in range(2):
                r = rp * 2 + p
                _row_copy(x_hbm, buf.at[slot], sem.at[slot], tok_ref[block * bm + r], r).start(priority=p)
            return c

        lax.fori_loop(0, bm // 2, body, 0, unroll=4)

    @pl.when(i == 0)
    def _():
        issue(i)

    @pl.when(i + 1 < n_used)
    def _():
        issue(i + 1)

    @pl.when(i < n_used)
    def _():
        slot = i % 2
        pltpu.make_async_copy(x_hbm.at[pl.ds(0, bm)], buf.at[slot], sem.at[slot]).wait()
        o_ref[...] = buf[slot].astype(o_ref.dtype)

    @pl.when(i >= n_used)
    def _():
        o_ref[...] = jnp.zeros_like(o_ref)


def _gather_rows(row_tok, n_used, xn, *, bm):
    n_rows = row_tok.shape[0]
    d = xn.shape[1]
    kern = functools.partial(_gather_kernel, bm=bm)
    return pl.pallas_call(
        kern,
        grid_spec=pltpu.PrefetchScalarGridSpec(
            num_scalar_prefetch=2,
            grid=(n_rows // bm,),
            in_specs=[pl.BlockSpec(memory_space=pl.ANY)],
            out_specs=pl.BlockSpec((bm, d), lambda i, tok, nu: (i, 0)),
            scratch_shapes=[pltpu.VMEM((2, bm, d), F32), pltpu.SemaphoreType.DMA((2,))]),
        out_shape=jax.ShapeDtypeStruct((n_rows, d), BF16),
        compiler_params=_cparams(1),
        name="gather_rows",
    )(row_tok, n_used, xn)


def _weight_ring_step(sched, nj, copies):
    be, _, first, group, next_expert, n_groups = sched
    j, i = pl.program_id(0), pl.program_id(1)
    seq = j * n_groups[0] + group[i]
    slot = seq % 2

    @pl.when(first[i] == 1)
    def _():
        @pl.when(seq == 0)
        def _():
            for c in copies(be[i], j, slot):
                c.start()

        last_group = group[i] == n_groups[0] - 1

        @pl.when(jnp.logical_not(jnp.logical_and(last_group, j == nj - 1)))
        def _():
            for c in copies(next_expert[i], j + last_group.astype(jnp.int32), 1 - slot):
                c.start()

        for c in copies(be[i], j, slot):
            c.wait()

    return slot


def _expert_up_kernel(*refs, tf, d_ff, nj):
    sched, (xs_ref, w_hbm, bg_ref, bl_ref, h_ref, wbuf, sem) = refs[:6], refs[6:]
    n_used = sched[1]
    i = pl.program_id(1)

    def copies(e, j, slot):
        col = pl.multiple_of(j * tf, tf)
        return (pltpu.make_async_copy(w_hbm.at[e, :, pl.ds(col, tf)], wbuf.at[slot, 0], sem.at[slot]),
                pltpu.make_async_copy(w_hbm.at[e, :, pl.ds(d_ff + col, tf)], wbuf.at[slot, 1], sem.at[slot]))

    @pl.when(i < n_used[0])
    def _():
        slot = _weight_ring_step(sched, nj, copies)
        x = xs_ref[...]
        g = jnp.dot(x, wbuf[slot, 0].astype(BF16), preferred_element_type=F32) + bg_ref[0]
        l = jnp.dot(x, wbuf[slot, 1].astype(BF16), preferred_element_type=F32) + bl_ref[0]
        glu = jnp.minimum(g, SWIGLU_LIMIT)
        lin = jnp.clip(l, -SWIGLU_LIMIT, SWIGLU_LIMIT)
        h_ref[...] = (glu * jax.nn.sigmoid(SWIGLU_ALPHA * glu) * (lin + 1.0)).astype(h_ref.dtype)

    @pl.when(i >= n_used[0])
    def _():
        h_ref[...] = jnp.zeros_like(h_ref)


def _expert_up(sched, xs, w_gate_up, b_gate_up, *, bm, tf):
    n_rows, d = xs.shape
    n_e, _, two_ff = w_gate_up.shape
    d_ff = two_ff // 2
    nj = d_ff // tf

    def blk(i, s):
        return jnp.minimum(i, s[1][0] - 1)

    kern = functools.partial(_expert_up_kernel, tf=tf, d_ff=d_ff, nj=nj)
    return pl.pallas_call(
        kern,
        grid_spec=pltpu.PrefetchScalarGridSpec(
            num_scalar_prefetch=len(sched),
            grid=(nj, n_rows // bm),
            in_specs=[pl.BlockSpec((bm, d), lambda j, i, *s: (blk(i, s), 0)),
                      pl.BlockSpec(memory_space=pl.ANY),
                      pl.BlockSpec((1, 1, tf), lambda j, i, *s: (s[0][blk(i, s)], 0, j)),
                      pl.BlockSpec((1, 1, tf), lambda j, i, *s: (s[0][blk(i, s)], 0, nj + j))],
            out_specs=pl.BlockSpec((bm, tf), lambda j, i, *s: (i, j)),
            scratch_shapes=[pltpu.VMEM((2, 2, d, tf), F32), pltpu.SemaphoreType.DMA((2,))]),
        out_shape=jax.ShapeDtypeStruct((n_rows, d_ff), BF16),
        compiler_params=_cparams(2),
        name="expert_up",
    )(*sched, xs, w_gate_up, b_gate_up.reshape(n_e, 1, two_ff), b_gate_up.reshape(n_e, 1, two_ff))


def _expert_down_kernel(*refs, tn, nj):
    sched, (h_ref, w_hbm, b_ref, y_ref, wbuf, sem) = refs[:6], refs[6:]
    n_used = sched[1]
    i = pl.program_id(1)

    def copies(e, j, slot):
        col = pl.multiple_of(j * tn, tn)
        return (pltpu.make_async_copy(w_hbm.at[e, :, pl.ds(col, tn)], wbuf.at[slot], sem.at[slot]),)

    @pl.when(i < n_used[0])
    def _():
        slot = _weight_ring_step(sched, nj, copies)
        y_ref[...] = (jnp.dot(h_ref[...], wbuf[slot].astype(BF16), preferred_element_type=F32)
                      + b_ref[0]).astype(y_ref.dtype)

    @pl.when(i >= n_used[0])
    def _():
        y_ref[...] = jnp.zeros_like(y_ref)


def _expert_down(sched, hidden, w_down, b_down, *, bm, tn):
    n_rows, d_ff = hidden.shape
    n_e, _, d = w_down.shape
    nj = d // tn

    def blk(i, s):
        return jnp.minimum(i, s[1][0] - 1)

    kern = functools.partial(_expert_down_kernel, tn=tn, nj=nj)
    return pl.pallas_call(
        kern,
        grid_spec=pltpu.PrefetchScalarGridSpec(
            num_scalar_prefetch=len(sched),
            grid=(nj, n_rows // bm),
            in_specs=[pl.BlockSpec((bm, d_ff), lambda j, i, *s: (blk(i, s), 0)),
                      pl.BlockSpec(memory_space=pl.ANY),
                      pl.BlockSpec((1, 1, tn), lambda j, i, *s: (s[0][blk(i, s)], 0, j))],
            out_specs=pl.BlockSpec((bm, tn), lambda j, i, *s: (i, j)),
            scratch_shapes=[pltpu.VMEM((2, d_ff, tn), F32), pltpu.SemaphoreType.DMA((2,))]),
        out_shape=jax.ShapeDtypeStruct((n_rows, d), F32),
        compiler_params=_cparams(2),
        name="expert_down",
    )(*sched, hidden, w_down, b_down.reshape(n_e, 1, d))


def _combine_kernel(dest_ref, h_ref, wt_ref, ys_hbm, g_ref, o_ref, buf, sem, *, tm):
    i = pl.program_id(0)

    def issue(tile):
        slot = tile % 2

        def body(r, c):
            for k in range(TOP_K):
                _row_copy(ys_hbm, buf.at[slot, k], sem.at[slot],
                          dest_ref[(tile * tm + r) * TOP_K + k], r).start(priority=k % 2)
            return c

        lax.fori_loop(0, tm, body, 0, unroll=4)

    @pl.when(i == 0)
    def _():
        issue(i)

    @pl.when(i + 1 < pl.num_programs(0))
    def _():
        issue(i + 1)

    slot = i % 2
    for k in range(TOP_K):
        pltpu.make_async_copy(ys_hbm.at[pl.ds(0, tm)], buf.at[slot, k], sem.at[slot]).wait()
    wt = wt_ref[...]
    y = h_ref[...]
    for k in range(TOP_K):
        y = y + wt[:, k:k + 1] * buf[slot, k]
    ms = jnp.mean(y * y, axis=-1, keepdims=True)
    o_ref[...] = y * lax.rsqrt(ms + EPS) * g_ref[...]


def _combine(dest_flat, h, wts, ys, g, *, tm):
    t, d = h.shape
    kern = functools.partial(_combine_kernel, tm=tm)
    return pl.pallas_call(
        kern,
        grid_spec=pltpu.PrefetchScalarGridSpec(
            num_scalar_prefetch=1,
            grid=(t // tm,),
            in_specs=[pl.BlockSpec((tm, d), lambda i, dst: (i, 0)),
                      pl.BlockSpec((tm, V7X_LANES), lambda i, dst: (i, 0)),
                      pl.BlockSpec(memory_space=pl.ANY),
                      pl.BlockSpec((1, d), lambda i, dst: (0, 0))],
            out_specs=pl.BlockSpec((tm, d), lambda i, dst: (i, 0)),
            scratch_shapes=[pltpu.VMEM((2, TOP_K, tm, d), F32), pltpu.SemaphoreType.DMA((2,))]),
        out_shape=jax.ShapeDtypeStruct((t, d), F32),
        compiler_params=_cparams(1),
        name="combine",
    )(dest_flat, h, wts, ys, g.reshape(1, d))


class _Tiles:
    norm_rows = 256
    mm_rows = 1024
    mm_cols = 512
    chunk_q = 256
    chunk_heads = 2
    stick_q = 256
    stick_heads = 4
    router_rows = 256
    moe_block = 256
    up_cols = 512
    down_cols = 2048
    combine_rows = 128


def _moe_layout(idx, rank, counts, *, bm):
    t = idx.shape[0]
    n_assign = t * TOP_K
    n_blocks = n_assign // bm + N_EXPERTS
    padded = (counts + bm - 1) // bm * bm
    padded_end = jnp.cumsum(padded)
    padded_start = padded_end - padded
    dest = (padded_start[idx] + rank).reshape(-1).astype(jnp.int32)
    tok = jnp.arange(n_assign, dtype=jnp.int32) // TOP_K
    row_tok = jnp.zeros((n_blocks * bm,), jnp.int32).at[dest].set(tok)
    blk_start = jnp.arange(n_blocks, dtype=jnp.int32) * bm
    block_expert = jnp.minimum(
        jnp.sum((padded_end[None, :] <= blk_start[:, None]).astype(jnp.int32), axis=1), N_EXPERTS - 1)
    n_used = (padded_end[-1:] // bm).astype(jnp.int32)
    has = padded > 0
    e_ids = jnp.arange(N_EXPERTS, dtype=jnp.int32)
    later = has[None, :] & (e_ids[None, :] > e_ids[:, None])
    first_e = jnp.min(jnp.where(has, e_ids, N_EXPERTS))
    next_later = jnp.min(jnp.where(later, e_ids[None, :], N_EXPERTS), axis=1)
    next_of_expert = jnp.where(next_later < N_EXPERTS, next_later, first_e)
    group_of_expert = jnp.cumsum(has.astype(jnp.int32)) - 1
    used = blk_start < padded_end[-1]
    first = (used & (blk_start == padded_start[block_expert])).astype(jnp.int32)
    sched = (block_expert.astype(jnp.int32), n_used, first,
             group_of_expert[block_expert].astype(jnp.int32),
             next_of_expert[block_expert].astype(jnp.int32),
             jnp.sum(has.astype(jnp.int32)).reshape(1))
    return dest, row_tok, sched


def kernel(x, norm_mix_g, w_in, b_gate, rel_bias, w_branch_a, w_branch_b, w_out, norm_ffn_g,
           w_router, b_router, w_gate_up, b_gate_up, w_down, b_down, norm_final_g):
    batch, seq, d = x.shape
    t = batch * seq
    tl = _Tiles
    width_a = N_HEADS_A * HEAD_DIM
    width_b = N_HEADS_B * HEAD_DIM
    qkv_width = 3 * width_a + 3 * width_b
    xt = x.reshape(t, d)

    xn = _rmsnorm(xt, norm_mix_g, tm=tl.norm_rows, out_dtype=BF16)
    qkv = _mm(xn, w_in, col_off=0, n_out=qkv_width, tm=tl.mm_rows, tn=tl.mm_cols,
              out_dtype=BF16, name="proj_qkv")
    gates = _mm(xn, w_in, col_off=qkv_width, n_out=2 * d, tm=tl.mm_rows, tn=tl.mm_cols,
                out_dtype=BF16, bias=b_gate, act="sigmoid", name="proj_gates")
    att_a = _chunk_attention(qkv, _chunk_bias_table(rel_bias, tl.chunk_q),
                             batch=batch, seq=seq, tq=tl.chunk_q, heads=tl.chunk_heads)
    att_b = _stick_attention(qkv, batch=batch, seq=seq, tq=tl.stick_q, heads=tl.stick_heads,
                             col_off=3 * width_a)
    merged = _branch_merge(att_a, att_b, w_branch_a, w_branch_b, gates, tm=tl.mm_rows, tn=tl.mm_cols)
    h = _mm(merged, w_out, col_off=0, n_out=d, tm=tl.mm_rows, tn=tl.mm_cols,
            out_dtype=F32, res=xt, name="out_proj")

    xn2, idx_l, wt_l, rank_l, cnt = _router(h, norm_ffn_g, w_router, b_router, tm=tl.router_rows)
    counts = cnt[0, :N_EXPERTS].astype(jnp.int32)
    dest, row_tok, sched = _moe_layout(idx_l[:, :TOP_K], rank_l[:, :TOP_K], counts, bm=tl.moe_block)
    xs = _gather_rows(row_tok, sched[1], xn2, bm=tl.moe_block)
    hidden = _expert_up(sched, xs, w_gate_up, b_gate_up, bm=tl.moe_block, tf=tl.up_cols)
    ys = _expert_down(sched, hidden, w_down, b_down, bm=tl.moe_block, tn=tl.down_cols)
    y = _combine(dest, h, wt_l, ys, norm_final_g, tm=tl.combine_rows)
    return y.reshape(batch, seq, d)
```

```python
import functools
import math

import jax
import jax.numpy as jnp
import numpy as np
from jax import lax
from jax.experimental import pallas as pl
from jax.experimental.pallas import tpu as pltpu

F32 = jnp.float32
BF16 = jnp.bfloat16

HEAD_DIM = 128
N_HEADS_A = 16
N_HEADS_B = 16
CHUNK = 64
LEFT_CHUNKS = 8
REL_CLIP = 256
N_EXPERTS = 32
TOP_K = 4
SWIGLU_ALPHA = 1.702
SWIGLU_LIMIT = 7.0
EPS = 1e-5
NEG = -1e30

V7X_LANES = 128
V7X_VMEM_BYTES = 64 * 1024 * 1024
VMEM_LIMIT = V7X_VMEM_BYTES - 8 * 1024 * 1024


def _cparams(n_axes):
    return pltpu.CompilerParams(
        dimension_semantics=("arbitrary",) * n_axes, vmem_limit_bytes=VMEM_LIMIT)


def _rmsnorm_kernel(x_ref, g_ref, o_ref):
    x = x_ref[...]
    ms = jnp.mean(x * x, axis=-1, keepdims=True)
    o_ref[...] = (x * lax.rsqrt(ms + EPS) * g_ref[...]).astype(o_ref.dtype)


def _rmsnorm(x, g, *, tm, out_dtype):
    t, d = x.shape
    return pl.pallas_call(
        _rmsnorm_kernel,
        grid=(t // tm,),
        in_specs=[pl.BlockSpec((tm, d), lambda i: (i, 0)),
                  pl.BlockSpec((1, d), lambda i: (0, 0))],
        out_specs=pl.BlockSpec((tm, d), lambda i: (i, 0)),
        out_shape=jax.ShapeDtypeStruct((t, d), out_dtype),
        compiler_params=_cparams(1),
        name="rmsnorm",
    )(x, g.reshape(1, d))


def _mm_kernel(*refs, has_bias, has_res, act):
    a_ref, w_ref = refs[0], refs[1]
    pos = 2
    b_ref = r_ref = None
    if has_bias:
        b_ref = refs[pos]
        pos += 1
    if has_res:
        r_ref = refs[pos]
        pos += 1
    o_ref = refs[pos]

    acc = jnp.dot(a_ref[...], w_ref[...].astype(BF16), preferred_element_type=F32)
    if has_bias:
        acc = acc + b_ref[...]
    if act == "sigmoid":
        acc = jax.nn.sigmoid(acc)
    if has_res:
        acc = acc + r_ref[...]
    o_ref[...] = acc.astype(o_ref.dtype)


def _mm(a, w, *, col_off, n_out, tm, tn, out_dtype, bias=None, res=None, act=None, name):
    m, k = a.shape
    assert w.shape[0] == k and col_off % tn == 0 and n_out % tn == 0 and m % tm == 0
    off = col_off // tn
    in_specs = [pl.BlockSpec((tm, k), lambda j, i: (i, 0)),
                pl.BlockSpec((k, tn), lambda j, i: (0, j + off))]
    args = [a, w]
    if bias is not None:
        in_specs.append(pl.BlockSpec((1, tn), lambda j, i: (0, j)))
        args.append(bias.reshape(1, n_out))
    if res is not None:
        in_specs.append(pl.BlockSpec((tm, tn), lambda j, i: (i, j)))
        args.append(res)
    kern = functools.partial(_mm_kernel, has_bias=bias is not None, has_res=res is not None, act=act)
    return pl.pallas_call(
        kern,
        grid=(n_out // tn, m // tm),
        in_specs=in_specs,
        out_specs=pl.BlockSpec((tm, tn), lambda j, i: (i, j)),
        out_shape=jax.ShapeDtypeStruct((m, n_out), out_dtype),
        compiler_params=_cparams(2),
        name=name,
    )(*args)


def _branch_kernel(a_ref, b_ref, wa_ref, wb_ref, ga_ref, gb_ref, o_ref):
    ya = jnp.dot(a_ref[...], wa_ref[...].astype(BF16), preferred_element_type=F32)
    yb = jnp.dot(b_ref[...], wb_ref[...].astype(BF16), preferred_element_type=F32)
    o_ref[...] = (ga_ref[...].astype(F32) * ya + gb_ref[...].astype(F32) * yb).astype(o_ref.dtype)


def _branch_merge(att_a, att_b, wa, wb, gates, *, tm, tn):
    m, ka = att_a.shape
    kb = att_b.shape[1]
    d = wa.shape[1]
    nj = d // tn
    return pl.pallas_call(
        _branch_kernel,
        grid=(nj, m // tm),
        in_specs=[pl.BlockSpec((tm, ka), lambda j, i: (i, 0)),
                  pl.BlockSpec((tm, kb), lambda j, i: (i, 0)),
                  pl.BlockSpec((ka, tn), lambda j, i: (0, j)),
                  pl.BlockSpec((kb, tn), lambda j, i: (0, j)),
                  pl.BlockSpec((tm, tn), lambda j, i: (i, j)),
                  pl.BlockSpec((tm, tn), lambda j, i: (i, j + nj))],
        out_specs=pl.BlockSpec((tm, tn), lambda j, i: (i, j)),
        out_shape=jax.ShapeDtypeStruct((m, d), BF16),
        compiler_params=_cparams(2),
        name="branch_merge",
    )(att_a, att_b, wa, wb, gates, gates)


def _chunk_bias_table(rel_bias, tq):
    left = LEFT_CHUNKS * CHUNK
    nk = 3 * tq
    assert left == 2 * tq
    qi = np.arange(tq)[:, None]
    kw = np.arange(nk)[None, :]
    krel = kw - left
    qc = qi // CHUNK
    kc = np.floor_divide(krel, CHUNK)
    valid = (kc >= qc - LEFT_CHUNKS) & (kc <= qc)
    rel_min, rel_max = left - nk + 1, left + tq - 1
    assert -REL_CLIP <= rel_min and rel_max > REL_CLIP
    rb = rel_bias.astype(F32).T
    nh = rb.shape[0]
    by_rel = jnp.concatenate(
        [rb[:, rel_min + REL_CLIP:], jnp.broadcast_to(rb[:, -1:], (nh, rel_max - REL_CLIP))], axis=1)
    period = rel_max - rel_min + 1
    w = jnp.concatenate([by_rel[:, :nk][:, ::-1], by_rel[:, nk:][:, ::-1]], axis=1)
    bias = jnp.tile(w, (1, tq))[:, :tq * (period - 1)].reshape(nh, tq, period - 1)[:, :, :nk]
    return jnp.where(jnp.asarray(valid)[None], bias, NEG)


def _chunk_attn_kernel(q_ref, k_ref, v_ref, bias_ref, o_ref, *, tq, heads, scale):
    i = pl.program_id(2)
    starts = [pl.multiple_of(jnp.maximum(i - 2 + j, 0) * tq, tq) for j in range(3)]
    for h in range(heads):
        hs = slice(h * HEAD_DIM, (h + 1) * HEAD_DIM)
        q = q_ref[:, hs]
        s_blocks = []
        for j in range(3):
            k = k_ref[pl.ds(starts[j], tq), hs]
            s = lax.dot_general(q, k, (((1,), (1,)), ((), ())), preferred_element_type=F32) * scale
            s = s + bias_ref[h, :, j * tq:(j + 1) * tq]
            if j < 2:
                s = jnp.where(i - 2 + j >= 0, s, NEG)
            s_blocks.append(s)
        m = jnp.maximum(jnp.maximum(jnp.max(s_blocks[0], axis=-1, keepdims=True),
                                    jnp.max(s_blocks[1], axis=-1, keepdims=True)),
                        jnp.max(s_blocks[2], axis=-1, keepdims=True))
        l = jnp.zeros_like(m)
        acc = jnp.zeros((tq, HEAD_DIM), F32)
        for j in range(3):
            p = jnp.exp(s_blocks[j] - m)
            l = l + jnp.sum(p, axis=-1, keepdims=True)
            v = v_ref[pl.ds(starts[j], tq), hs]
            acc = acc + jnp.dot(p.astype(BF16), v, preferred_element_type=F32)
        o_ref[:, hs] = (acc / l).astype(o_ref.dtype)


def _chunk_attention(qkv, bias_tab, *, batch, seq, tq, heads):
    nq = seq // tq
    nh = N_HEADS_A
    assert nh % heads == 0
    ng = nh // heads
    width = heads * HEAD_DIM
    kern = functools.partial(_chunk_attn_kernel, tq=tq, heads=heads, scale=1.0 / math.sqrt(HEAD_DIM))
    return pl.pallas_call(
        kern,
        grid=(batch, ng, nq),
        in_specs=[pl.BlockSpec((tq, width), lambda b, g, i: (b * nq + i, g)),
                  pl.BlockSpec((seq, width), lambda b, g, i: (b, ng + g)),
                  pl.BlockSpec((seq, width), lambda b, g, i: (b, 2 * ng + g)),
                  pl.BlockSpec((heads, tq, 3 * tq), lambda b, g, i: (g, 0, 0))],
        out_specs=pl.BlockSpec((tq, width), lambda b, g, i: (b * nq + i, g)),
        out_shape=jax.ShapeDtypeStruct((batch * seq, nh * HEAD_DIM), BF16),
        compiler_params=_cparams(3),
        name="chunk_attention",
    )(qkv, qkv, qkv, bias_tab)


def _suffix_matrix(tk):
    j = np.arange(tk)[:, None]
    s = np.arange(tk)[None, :]
    later = -(j > s).astype(np.float32)
    half = np.concatenate([later, -np.ones((tk, tk), np.float32)], axis=1)
    return jnp.asarray(np.concatenate([half, half], axis=0), BF16)


def _softplus(z):
    return jnp.maximum(z, 0.0) + jnp.log(1.0 + jnp.exp(-jnp.abs(z)))


def _stick_kernel(q_ref, k_ref, v_ref, u_ref, o_ref, acc_ref, run_ref, *, tq, tk, heads, scale):
    i = pl.program_id(2)
    u = u_ref[...]
    row = lax.broadcasted_iota(jnp.int32, (tq, tq), 0)
    col = lax.broadcasted_iota(jnp.int32, (tq, tq), 1)
    causal = col < row

    def span(h, start, diagonal):
        hs = slice(h * HEAD_DIM, (h + 1) * HEAD_DIM)
        q = q_ref[:, hs]
        k = k_ref[pl.ds(start, tq), hs]
        v = v_ref[pl.ds(start, tq), hs]
        z = lax.dot_general(q, k, (((1,), (1,)), ((), ())), preferred_element_type=F32) * scale
        sp = _softplus(z)
        neg_log_keep = jnp.where(causal, sp, 0.0) if diagonal else sp
        run = run_ref[h]
        later = [None, None]
        for half in (1, 0):
            s = neg_log_keep[:, half * tk:(half + 1) * tk]
            hi = s.astype(BF16)
            lo = (s - hi.astype(F32)).astype(BF16)
            sums = jnp.dot(jnp.concatenate([hi, lo], axis=1), u, preferred_element_type=F32)
            later[half] = sums[:, :tk] + run
            run = run + sums[:, tk:]
        run_ref[h] = run
        a = jnp.exp(z - sp + jnp.concatenate(later, axis=1))
        if diagonal:
            a = jnp.where(causal, a, 0.0)
        acc_ref[h] += jnp.dot(a.astype(BF16), v, preferred_element_type=F32)

    acc_ref[...] = jnp.zeros_like(acc_ref)
    run_ref[...] = jnp.zeros_like(run_ref)
    for h in range(heads):
        span(h, pl.multiple_of(i * tq, tq), True)

    def body(t, c):
        start = pl.multiple_of((i - 1 - t) * tq, tq)
        for h in range(heads):
            span(h, start, False)
        return c

    lax.fori_loop(0, i, body, 0)
    for h in range(heads):
        o_ref[:, h * HEAD_DIM:(h + 1) * HEAD_DIM] = acc_ref[h].astype(o_ref.dtype)


def _stick_attention(qkv, *, batch, seq, tq, heads, col_off):
    nq = seq // tq
    nh = N_HEADS_B
    tk = tq // 2
    width = heads * HEAD_DIM
    assert col_off % width == 0 and nh % heads == 0
    c0 = col_off // width
    ng = nh // heads
    kern = functools.partial(_stick_kernel, tq=tq, tk=tk, heads=heads, scale=1.0 / math.sqrt(HEAD_DIM))
    return pl.pallas_call(
        kern,
        grid=(batch, ng, nq),
        in_specs=[pl.BlockSpec((tq, width), lambda b, g, i: (b * nq + i, c0 + g)),
                  pl.BlockSpec((seq, width), lambda b, g, i: (b, c0 + ng + g)),
                  pl.BlockSpec((seq, width), lambda b, g, i: (b, c0 + 2 * ng + g)),
                  pl.BlockSpec((2 * tk, 2 * tk), lambda b, g, i: (0, 0))],
        out_specs=pl.BlockSpec((tq, width), lambda b, g, i: (b * nq + i, g)),
        out_shape=jax.ShapeDtypeStruct((batch * seq, nh * HEAD_DIM), BF16),
        scratch_shapes=[pltpu.VMEM((heads, tq, HEAD_DIM), F32), pltpu.VMEM((heads, tq, tk), F32)],
        compiler_params=_cparams(3),
        name="stick_attention",
    )(qkv, qkv, qkv, _suffix_matrix(tk))


def _router_kernel(h_ref, g_ref, wr_ref, br_ref, xn_ref, idx_ref, wt_ref, rank_ref, cnt_ref,
                   wr_hi, wr_lo, carry, *, tm):
    step = pl.program_id(0)

    @pl.when(step == 0)
    def _():
        w = wr_ref[...]
        hi = w.astype(BF16)
        wr_hi[...] = hi
        wr_lo[...] = (w - hi.astype(F32)).astype(BF16)
        carry[...] = jnp.zeros_like(carry)

    x = h_ref[...]
    ms = jnp.mean(x * x, axis=-1, keepdims=True)
    xn = x * lax.rsqrt(ms + EPS) * g_ref[...]
    xn_ref[...] = xn.reshape(xn_ref.shape).astype(xn_ref.dtype)
    xh = xn.astype(BF16)
    xl = (xn - xh.astype(F32)).astype(BF16)
    logits = (jnp.dot(xh, wr_hi[...], preferred_element_type=F32)
              + jnp.dot(xh, wr_lo[...], preferred_element_type=F32)
              + jnp.dot(xl, wr_hi[...], preferred_element_type=F32)) + br_ref[...]
    lane = lax.broadcasted_iota(jnp.int32, (tm, V7X_LANES), 1)
    logits = jnp.where(lane < N_EXPERTS, logits, -jnp.inf)

    r = lax.broadcasted_iota(jnp.int32, (tm, tm), 0)
    c = lax.broadcasted_iota(jnp.int32, (tm, tm), 1)
    tri = (c < r).astype(BF16)

    sel_mask = jnp.zeros((tm, V7X_LANES), F32)
    onehots, vals = [], []
    work = logits
    for _ in range(TOP_K):
        mx = jnp.max(work, axis=-1, keepdims=True)
        first = jnp.min(jnp.where(work == mx, lane, V7X_LANES), axis=-1, keepdims=True)
        oh = lane == first
        onehots.append(oh)
        vals.append(mx)
        sel_mask = sel_mask + oh.astype(F32)
        work = jnp.where(oh, -jnp.inf, work)

    prefix = jnp.dot(tri, sel_mask.astype(BF16), preferred_element_type=F32) + carry[...]
    carry[...] = carry[...] + jnp.sum(sel_mask, axis=0, keepdims=True)
    cnt_ref[...] = jnp.broadcast_to(carry[...], cnt_ref.shape)

    es = [jnp.exp(v - vals[0]) for v in vals]
    denom = es[0] + es[1] + es[2] + es[3]
    idx_out = jnp.zeros((tm, V7X_LANES), jnp.int32)
    wt_out = jnp.zeros((tm, V7X_LANES), F32)
    rank_out = jnp.zeros((tm, V7X_LANES), jnp.int32)
    for k in range(TOP_K):
        oh = onehots[k]
        e_k = jnp.sum(jnp.where(oh, lane, 0), axis=-1, keepdims=True)
        r_k = jnp.sum(jnp.where(oh, prefix, 0.0), axis=-1, keepdims=True).astype(jnp.int32)
        idx_out = jnp.where(lane == k, e_k, idx_out)
        wt_out = jnp.where(lane == k, es[k] / denom, wt_out)
        rank_out = jnp.where(lane == k, r_k, rank_out)
    idx_ref[...] = idx_out
    wt_ref[...] = wt_out
    rank_ref[...] = rank_out


def _router(h, g, w_router, b_router, *, tm):
    t, d = h.shape
    wr = jnp.zeros((d, V7X_LANES), F32).at[:, :N_EXPERTS].set(w_router)
    br = jnp.zeros((1, V7X_LANES), F32).at[0, :N_EXPERTS].set(b_router)
    row = lambda i: (i, 0)
    fixed = lambda i: (0, 0)
    kern = functools.partial(_router_kernel, tm=tm)
    return pl.pallas_call(
        kern,
        grid=(t // tm,),
        in_specs=[pl.BlockSpec((tm, d), row),
                  pl.BlockSpec((1, d), fixed),
                  pl.BlockSpec((d, V7X_LANES), fixed),
                  pl.BlockSpec((1, V7X_LANES), fixed)],
        out_specs=[pl.BlockSpec((tm, d // V7X_LANES, V7X_LANES), lambda i: (i, 0, 0)),
                   pl.BlockSpec((tm, V7X_LANES), row),
                   pl.BlockSpec((tm, V7X_LANES), row),
                   pl.BlockSpec((tm, V7X_LANES), row),
                   pl.BlockSpec((8, V7X_LANES), fixed)],
        out_shape=[jax.ShapeDtypeStruct((t, d // V7X_LANES, V7X_LANES), BF16),
                   jax.ShapeDtypeStruct((t, V7X_LANES), jnp.int32),
                   jax.ShapeDtypeStruct((t, V7X_LANES), F32),
                   jax.ShapeDtypeStruct((t, V7X_LANES), jnp.int32),
                   jax.ShapeDtypeStruct((8, V7X_LANES), F32)],
        scratch_shapes=[pltpu.VMEM((d, V7X_LANES), BF16),
                        pltpu.VMEM((d, V7X_LANES), BF16),
                        pltpu.VMEM((1, V7X_LANES), F32)],
        compiler_params=_cparams(1),
        name="router",
    )(h, g.reshape(1, d), wr, br)


def _row_copy(src_hbm, dst, sem, src_row, dst_row):
    return pltpu.make_async_copy(src_hbm.at[pl.ds(src_row, 1)], dst.at[pl.ds(dst_row, 1)], sem)


def _gather_kernel(tok_ref, nu_ref, x_hbm, o_ref, buf, sem, *, bm):
    i = pl.program_id(0)
    n_used = nu_ref[0]

    def issue(block):
        slot = block % 2

        def body(rp, c):
            for p in range(2):
                r = rp * 2 + p
                _row_copy(x_hbm, buf.at[slot], sem.at[slot], tok_ref[block * bm + r], r).start(priority=p)
            return c

        lax.fori_loop(0, bm // 2, body, 0, unroll=4)

    @pl.when(i == 0)
    def _():
        issue(i)

    @pl.when(i + 1 < n_used)
    def _():
        issue(i + 1)

    @pl.when(i < n_used)
    def _():
        slot = i % 2
        pltpu.make_async_copy(x_hbm.at[pl.ds(0, bm)], buf.at[slot], sem.at[slot]).wait()
        o_ref[...] = buf[slot].reshape(o_ref.shape)

    @pl.when(i >= n_used)
    def _():
        o_ref[...] = jnp.zeros_like(o_ref)


def _gather_rows(row_tok, n_used, xn, *, bm):
    n_rows = row_tok.shape[0]
    _, sub, lanes = xn.shape
    d = sub * lanes
    kern = functools.partial(_gather_kernel, bm=bm)
    return pl.pallas_call(
        kern,
        grid_spec=pltpu.PrefetchScalarGridSpec(
            num_scalar_prefetch=2,
            grid=(n_rows // bm,),
            in_specs=[pl.BlockSpec(memory_space=pl.ANY)],
            out_specs=pl.BlockSpec((bm, d), lambda i, tok, nu: (i, 0)),
            scratch_shapes=[pltpu.VMEM((2, bm, sub, lanes), xn.dtype), pltpu.SemaphoreType.DMA((2,))]),
        out_shape=jax.ShapeDtypeStruct((n_rows, d), xn.dtype),
        compiler_params=_cparams(1),
        name="gather_rows",
    )(row_tok, n_used, xn)


def _weight_ring_step(sched, nj, copies):
    be, _, first, group, next_expert, n_groups = sched
    j, i = pl.program_id(0), pl.program_id(1)
    seq = j * n_groups[0] + group[i]
    slot = seq % 2

    @pl.when(first[i] == 1)
    def _():
        @pl.when(seq == 0)
        def _():
            for c in copies(be[i], j, slot):
                c.start(priority=1)

        last_group = group[i] == n_groups[0] - 1

        @pl.when(jnp.logical_not(jnp.logical_and(last_group, j == nj - 1)))
        def _():
            for c in copies(next_expert[i], j + last_group.astype(jnp.int32), 1 - slot):
                c.start(priority=1)

        for c in copies(be[i], j, slot):
            c.wait()

    return slot


def _expert_up_kernel(*refs, tf, d_ff, nj):
    sched, (xs_ref, w_hbm, bg_ref, bl_ref, h_ref, wbuf, sem) = refs[:6], refs[6:]
    n_used = sched[1]
    i = pl.program_id(1)

    def copies(e, j, slot):
        col = pl.multiple_of(j * tf, tf)
        return (pltpu.make_async_copy(w_hbm.at[e, :, pl.ds(col, tf)], wbuf.at[slot, 0], sem.at[slot]),
                pltpu.make_async_copy(w_hbm.at[e, :, pl.ds(d_ff + col, tf)], wbuf.at[slot, 1], sem.at[slot]))

    @pl.when(i < n_used[0])
    def _():
        slot = _weight_ring_step(sched, nj, copies)
        x = xs_ref[...]
        g = jnp.dot(x, wbuf[slot, 0].astype(BF16), preferred_element_type=F32) + bg_ref[0]
        l = jnp.dot(x, wbuf[slot, 1].astype(BF16), preferred_element_type=F32) + bl_ref[0]
        glu = jnp.minimum(g, SWIGLU_LIMIT)
        lin = jnp.clip(l, -SWIGLU_LIMIT, SWIGLU_LIMIT)
        h_ref[...] = (glu * jax.nn.sigmoid(SWIGLU_ALPHA * glu) * (lin + 1.0)).astype(h_ref.dtype)

    @pl.when(i >= n_used[0])
    def _():
        h_ref[...] = jnp.zeros_like(h_ref)


def _expert_up(sched, xs, w_gate_up, b_gate_up, *, bm, tf):
    n_rows, d = xs.shape
    n_e, _, two_ff = w_gate_up.shape
    d_ff = two_ff // 2
    nj = d_ff // tf

    def blk(i, s):
        return jnp.minimum(i, s[1][0] - 1)

    kern = functools.partial(_expert_up_kernel, tf=tf, d_ff=d_ff, nj=nj)
    return pl.pallas_call(
        kern,
        grid_spec=pltpu.PrefetchScalarGridSpec(
            num_scalar_prefetch=len(sched),
            grid=(nj, n_rows // bm),
            in_specs=[pl.BlockSpec((bm, d), lambda j, i, *s: (blk(i, s), 0)),
                      pl.BlockSpec(memory_space=pl.ANY),
                      pl.BlockSpec((1, 1, tf), lambda j, i, *s: (s[0][blk(i, s)], 0, j)),
                      pl.BlockSpec((1, 1, tf), lambda j, i, *s: (s[0][blk(i, s)], 0, nj + j))],
            out_specs=pl.BlockSpec((bm, tf), lambda j, i, *s: (i, j)),
            scratch_shapes=[pltpu.VMEM((2, 2, d, tf), F32), pltpu.SemaphoreType.DMA((2,))]),
        out_shape=jax.ShapeDtypeStruct((n_rows, d_ff), BF16),
        compiler_params=_cparams(2),
        name="expert_up",
    )(*sched, xs, w_gate_up, b_gate_up.reshape(n_e, 1, two_ff), b_gate_up.reshape(n_e, 1, two_ff))


def _expert_down_kernel(*refs, tn, nj):
    sched, (h_ref, w_hbm, b_ref, y_ref, wbuf, sem) = refs[:6], refs[6:]
    n_used = sched[1]
    i = pl.program_id(1)

    def copies(e, j, slot):
        col = pl.multiple_of(j * tn, tn)
        return (pltpu.make_async_copy(w_hbm.at[e, :, pl.ds(col, tn)], wbuf.at[slot], sem.at[slot]),)

    @pl.when(i < n_used[0])
    def _():
        slot = _weight_ring_step(sched, nj, copies)
        y = jnp.dot(h_ref[...], wbuf[slot].astype(BF16), preferred_element_type=F32) + b_ref[0]
        y_ref[...] = y.reshape(y_ref.shape).astype(y_ref.dtype)

    @pl.when(i >= n_used[0])
    def _():
        y_ref[...] = jnp.zeros_like(y_ref)


def _expert_down(sched, hidden, w_down, b_down, *, bm, tn):
    n_rows, d_ff = hidden.shape
    n_e, _, d = w_down.shape
    nj = d // tn

    def blk(i, s):
        return jnp.minimum(i, s[1][0] - 1)

    kern = functools.partial(_expert_down_kernel, tn=tn, nj=nj)
    return pl.pallas_call(
        kern,
        grid_spec=pltpu.PrefetchScalarGridSpec(
            num_scalar_prefetch=len(sched),
            grid=(nj, n_rows // bm),
            in_specs=[pl.BlockSpec((bm, d_ff), lambda j, i, *s: (blk(i, s), 0)),
                      pl.BlockSpec(memory_space=pl.ANY),
                      pl.BlockSpec((1, 1, tn), lambda j, i, *s: (s[0][blk(i, s)], 0, j))],
            out_specs=pl.BlockSpec((bm, tn // V7X_LANES, V7X_LANES), lambda j, i, *s: (i, j, 0)),
            scratch_shapes=[pltpu.VMEM((2, d_ff, tn), F32), pltpu.SemaphoreType.DMA((2,))]),
        out_shape=jax.ShapeDtypeStruct((n_rows, d // V7X_LANES, V7X_LANES), BF16),
        compiler_params=_cparams(2),
        name="expert_down",
    )(*sched, hidden, w_down, b_down.reshape(n_e, 1, d))


def _combine_kernel(dest_ref, h_ref, wt_ref, ys_hbm, g_ref, o_ref, buf, sem, *, tm):
    i = pl.program_id(0)

    def issue(tile):
        slot = tile % 2

        def body(r, c):
            for k in range(TOP_K):
                _row_copy(ys_hbm, buf.at[slot, k], sem.at[slot],
                          dest_ref[(tile * tm + r) * TOP_K + k], r).start(priority=k % 2)
            return c

        lax.fori_loop(0, tm, body, 0, unroll=4)

    @pl.when(i == 0)
    def _():
        issue(i)

    @pl.when(i + 1 < pl.num_programs(0))
    def _():
        issue(i + 1)

    slot = i % 2
    for k in range(TOP_K):
        pltpu.make_async_copy(ys_hbm.at[pl.ds(0, tm)], buf.at[slot, k], sem.at[slot]).wait()
    wt = wt_ref[...]
    y = h_ref[...]
    for k in range(TOP_K):
        y = y + wt[:, k:k + 1] * buf[slot, k].reshape(y.shape).astype(F32)
    ms = jnp.mean(y * y, axis=-1, keepdims=True)
    o_ref[...] = y * lax.rsqrt(ms + EPS) * g_ref[...]


def _combine(dest_flat, h, wts, ys, g, *, tm):
    t, d = h.shape
    kern = functools.partial(_combine_kernel, tm=tm)
    return pl.pallas_call(
        kern,
        grid_spec=pltpu.PrefetchScalarGridSpec(
            num_scalar_prefetch=1,
            grid=(t // tm,),
            in_specs=[pl.BlockSpec((tm, d), lambda i, dst: (i, 0)),
                      pl.BlockSpec((tm, V7X_LANES), lambda i, dst: (i, 0)),
                      pl.BlockSpec(memory_space=pl.ANY),
                      pl.BlockSpec((1, d), lambda i, dst: (0, 0))],
            out_specs=pl.BlockSpec((tm, d), lambda i, dst: (i, 0)),
            scratch_shapes=[pltpu.VMEM((2, TOP_K, tm) + ys.shape[1:], ys.dtype),
                            pltpu.SemaphoreType.DMA((2,))]),
        out_shape=jax.ShapeDtypeStruct((t, d), F32),
        compiler_params=_cparams(1),
        name="combine",
    )(dest_flat, h, wts, ys, g.reshape(1, d))


class _Tiles:
    norm_rows = 256
    mm_rows = 1024
    mm_cols = 512
    chunk_q = 256
    chunk_heads = 4
    stick_q = 256
    stick_heads = 8
    router_rows = 256
    moe_block = 256
    up_cols = 512
    down_cols = 2048
    combine_rows = 256


def _moe_layout(idx, rank, counts, *, bm):
    t = idx.shape[0]
    n_assign = t * TOP_K
    n_blocks = n_assign // bm + N_EXPERTS
    padded = (counts + bm - 1) // bm * bm
    padded_end = jnp.cumsum(padded)
    padded_start = padded_end - padded
    dest = (padded_start[idx] + rank).reshape(-1).astype(jnp.int32)
    tok = jnp.arange(n_assign, dtype=jnp.int32) // TOP_K
    row_tok = jnp.zeros((n_blocks * bm,), jnp.int32).at[dest].set(tok)
    blk_start = jnp.arange(n_blocks, dtype=jnp.int32) * bm
    block_expert = jnp.minimum(
        jnp.sum((padded_end[None, :] <= blk_start[:, None]).astype(jnp.int32), axis=1), N_EXPERTS - 1)
    n_used = (padded_end[-1:] // bm).astype(jnp.int32)
    has = padded > 0
    e_ids = jnp.arange(N_EXPERTS, dtype=jnp.int32)
    later = has[None, :] & (e_ids[None, :] > e_ids[:, None])
    first_e = jnp.min(jnp.where(has, e_ids, N_EXPERTS))
    next_later = jnp.min(jnp.where(later, e_ids[None, :], N_EXPERTS), axis=1)
    next_of_expert = jnp.where(next_later < N_EXPERTS, next_later, first_e)
    group_of_expert = jnp.cumsum(has.astype(jnp.int32)) - 1
    used = blk_start < padded_end[-1]
    first = (used & (blk_start == padded_start[block_expert])).astype(jnp.int32)
    sched = (block_expert.astype(jnp.int32), n_used, first,
             group_of_expert[block_expert].astype(jnp.int32),
             next_of_expert[block_expert].astype(jnp.int32),
             jnp.sum(has.astype(jnp.int32)).reshape(1))
    return dest, row_tok, sched


def kernel(x, norm_mix_g, w_in, b_gate, rel_bias, w_branch_a, w_branch_b, w_out, norm_ffn_g,
           w_router, b_router, w_gate_up, b_gate_up, w_down, b_down, norm_final_g):
    batch, seq, d = x.shape
    t = batch * seq
    tl = _Tiles
    width_a = N_HEADS_A * HEAD_DIM
    width_b = N_HEADS_B * HEAD_DIM
    qkv_width = 3 * width_a + 3 * width_b
    xt = x.reshape(t, d)

    xn = _rmsnorm(xt, norm_mix_g, tm=tl.norm_rows, out_dtype=BF16)
    qkv = _mm(xn, w_in, col_off=0, n_out=qkv_width, tm=tl.mm_rows, tn=tl.mm_cols,
              out_dtype=BF16, name="proj_qkv")
    gates = _mm(xn, w_in, col_off=qkv_width, n_out=2 * d, tm=tl.mm_rows, tn=tl.mm_cols,
                out_dtype=BF16, bias=b_gate, act="sigmoid", name="proj_gates")
    att_a = _chunk_attention(qkv, _chunk_bias_table(rel_bias, tl.chunk_q),
                             batch=batch, seq=seq, tq=tl.chunk_q, heads=tl.chunk_heads)
    att_b = _stick_attention(qkv, batch=batch, seq=seq, tq=tl.stick_q, heads=tl.stick_heads,
                             col_off=3 * width_a)
    merged = _branch_merge(att_a, att_b, w_branch_a, w_branch_b, gates, tm=tl.mm_rows, tn=tl.mm_cols)
    h = _mm(merged, w_out, col_off=0, n_out=d, tm=tl.mm_rows, tn=tl.mm_cols,
            out_dtype=F32, res=xt, name="out_proj")

    xn2, idx_l, wt_l, rank_l, cnt = _router(h, norm_ffn_g, w_router, b_router, tm=tl.router_rows)
    counts = cnt[0, :N_EXPERTS].astype(jnp.int32)
    dest, row_tok, sched = _moe_layout(idx_l[:, :TOP_K], rank_l[:, :TOP_K], counts, bm=tl.moe_block)
    xs = _gather_rows(row_tok, sched[1], xn2, bm=tl.moe_block)
    hidden = _expert_up(sched, xs, w_gate_up, b_gate_up, bm=tl.moe_block, tf=tl.up_cols)
    ys = _expert_down(sched, hidden, w_down, b_down, bm=tl.moe_block, tn=tl.down_cols)
    y = _combine(dest, h, wt_l, ys, norm_final_g, tm=tl.combine_rows)
    return y.reshape(batch, seq, d)
```

```python
import functools
import math

import jax
import jax.numpy as jnp
import numpy as np
from jax import lax
from jax.experimental import pallas as pl
from jax.experimental.pallas import tpu as pltpu

F32 = jnp.float32
BF16 = jnp.bfloat16

HEAD_DIM = 128
N_HEADS_A = 16
N_HEADS_B = 16
CHUNK = 64
LEFT_CHUNKS = 8
REL_CLIP = 256
N_EXPERTS = 32
TOP_K = 4
TOP_K_SHIFT = 2
assert 1 << TOP_K_SHIFT == TOP_K
SWIGLU_ALPHA = 1.702
SWIGLU_LIMIT = 7.0
EPS = 1e-5
NEG = -1e30

V7X_LANES = 128
V7X_VMEM_BYTES = 64 * 1024 * 1024
VMEM_LIMIT = V7X_VMEM_BYTES - 8 * 1024 * 1024


def _cparams(n_axes):
    return pltpu.CompilerParams(
        dimension_semantics=("arbitrary",) * n_axes, vmem_limit_bytes=VMEM_LIMIT)


def _rmsnorm_kernel(x_ref, g_ref, o_ref):
    x = x_ref[...]
    ms = jnp.mean(x * x, axis=-1, keepdims=True)
    o_ref[...] = (x * lax.rsqrt(ms + EPS) * g_ref[...]).astype(o_ref.dtype)


def _rmsnorm(x, g, *, tm, out_dtype):
    t, d = x.shape
    return pl.pallas_call(
        _rmsnorm_kernel,
        grid=(t // tm,),
        in_specs=[pl.BlockSpec((tm, d), lambda i: (i, 0)),
                  pl.BlockSpec((1, d), lambda i: (0, 0))],
        out_specs=pl.BlockSpec((tm, d), lambda i: (i, 0)),
        out_shape=jax.ShapeDtypeStruct((t, d), out_dtype),
        compiler_params=_cparams(1),
        name="rmsnorm",
    )(x, g.reshape(1, d))


def _mm_kernel(*refs, has_bias, has_res, act):
    a_ref, w_ref = refs[0], refs[1]
    pos = 2
    b_ref = r_ref = None
    if has_bias:
        b_ref = refs[pos]
        pos += 1
    if has_res:
        r_ref = refs[pos]
        pos += 1
    o_ref = refs[pos]

    acc = jnp.dot(a_ref[...], w_ref[...].astype(BF16), preferred_element_type=F32)
    if has_bias:
        acc = acc + b_ref[...]
    if act == "sigmoid":
        acc = jax.nn.sigmoid(acc)
    if has_res:
        acc = acc + r_ref[...]
    o_ref[...] = acc.astype(o_ref.dtype)


def _mm(a, w, *, col_off, n_out, tm, tn, out_dtype, bias=None, res=None, act=None, name):
    m, k = a.shape
    assert w.shape[0] == k and col_off % tn == 0 and n_out % tn == 0 and m % tm == 0
    off = col_off // tn
    in_specs = [pl.BlockSpec((tm, k), lambda j, i: (i, 0)),
                pl.BlockSpec((k, tn), lambda j, i: (0, j + off))]
    args = [a, w]
    if bias is not None:
        in_specs.append(pl.BlockSpec((1, tn), lambda j, i: (0, j)))
        args.append(bias.reshape(1, n_out))
    if res is not None:
        in_specs.append(pl.BlockSpec((tm, tn), lambda j, i: (i, j)))
        args.append(res)
    kern = functools.partial(_mm_kernel, has_bias=bias is not None, has_res=res is not None, act=act)
    return pl.pallas_call(
        kern,
        grid=(n_out // tn, m // tm),
        in_specs=in_specs,
        out_specs=pl.BlockSpec((tm, tn), lambda j, i: (i, j)),
        out_shape=jax.ShapeDtypeStruct((m, n_out), out_dtype),
        compiler_params=_cparams(2),
        name=name,
    )(*args)


def _branch_kernel(a_ref, b_ref, wa_ref, wb_ref, ga_ref, gb_ref, o_ref):
    ya = jnp.dot(a_ref[...], wa_ref[...].astype(BF16), preferred_element_type=F32)
    yb = jnp.dot(b_ref[...], wb_ref[...].astype(BF16), preferred_element_type=F32)
    o_ref[...] = (ga_ref[...].astype(F32) * ya + gb_ref[...].astype(F32) * yb).astype(o_ref.dtype)


def _branch_merge(att_a, att_b, wa, wb, gates, *, tm, tn):
    m, ka = att_a.shape
    kb = att_b.shape[1]
    d = wa.shape[1]
    nj = d // tn
    return pl.pallas_call(
        _branch_kernel,
        grid=(nj, m // tm),
        in_specs=[pl.BlockSpec((tm, ka), lambda j, i: (i, 0)),
                  pl.BlockSpec((tm, kb), lambda j, i: (i, 0)),
                  pl.BlockSpec((ka, tn), lambda j, i: (0, j)),
                  pl.BlockSpec((kb, tn), lambda j, i: (0, j)),
                  pl.BlockSpec((tm, tn), lambda j, i: (i, j)),
                  pl.BlockSpec((tm, tn), lambda j, i: (i, j + nj))],
        out_specs=pl.BlockSpec((tm, tn), lambda j, i: (i, j)),
        out_shape=jax.ShapeDtypeStruct((m, d), BF16),
        compiler_params=_cparams(2),
        name="branch_merge",
    )(att_a, att_b, wa, wb, gates, gates)


def _chunk_bias_table(rel_bias, tq):
    left = LEFT_CHUNKS * CHUNK
    nk = 3 * tq
    assert left == 2 * tq
    qi = np.arange(tq)[:, None]
    kw = np.arange(nk)[None, :]
    krel = kw - left
    qc = qi // CHUNK
    kc = np.floor_divide(krel, CHUNK)
    valid = (kc >= qc - LEFT_CHUNKS) & (kc <= qc)
    rel_min, rel_max = left - nk + 1, left + tq - 1
    assert -REL_CLIP <= rel_min and rel_max > REL_CLIP
    rb = rel_bias.astype(F32).T
    nh = rb.shape[0]
    by_rel = jnp.concatenate(
        [rb[:, rel_min + REL_CLIP:], jnp.broadcast_to(rb[:, -1:], (nh, rel_max - REL_CLIP))], axis=1)
    period = rel_max - rel_min + 1
    w = jnp.concatenate([by_rel[:, :nk][:, ::-1], by_rel[:, nk:][:, ::-1]], axis=1)
    bias = jnp.tile(w, (1, tq))[:, :tq * (period - 1)].reshape(nh, tq, period - 1)[:, :, :nk]
    return jnp.where(jnp.asarray(valid)[None], bias, NEG)


def _chunk_attn_kernel(q_ref, k_ref, v_ref, bias_ref, o_ref, *, tq, heads, scale):
    i = pl.program_id(2)
    starts = [pl.multiple_of(jnp.maximum(i - 2 + j, 0) * tq, tq) for j in range(3)]
    for h in range(heads):
        hs = slice(h * HEAD_DIM, (h + 1) * HEAD_DIM)
        q = q_ref[:, hs]
        s_blocks = []
        for j in range(3):
            k = k_ref[pl.ds(starts[j], tq), hs]
            s = lax.dot_general(q, k, (((1,), (1,)), ((), ())), preferred_element_type=F32) * scale
            s = s + bias_ref[h, :, j * tq:(j + 1) * tq]
            if j < 2:
                s = jnp.where(i - 2 + j >= 0, s, NEG)
            s_blocks.append(s)
        m = jnp.maximum(jnp.maximum(jnp.max(s_blocks[0], axis=-1, keepdims=True),
                                    jnp.max(s_blocks[1], axis=-1, keepdims=True)),
                        jnp.max(s_blocks[2], axis=-1, keepdims=True))
        l = jnp.zeros_like(m)
        acc = jnp.zeros((tq, HEAD_DIM), F32)
        for j in range(3):
            p = jnp.exp(s_blocks[j] - m)
            l = l + jnp.sum(p, axis=-1, keepdims=True)
            v = v_ref[pl.ds(starts[j], tq), hs]
            acc = acc + jnp.dot(p.astype(BF16), v, preferred_element_type=F32)
        o_ref[:, hs] = (acc / l).astype(o_ref.dtype)


def _chunk_attention(qkv, bias_tab, *, batch, seq, tq, heads):
    nq = seq // tq
    nh = N_HEADS_A
    assert nh % heads == 0
    ng = nh // heads
    width = heads * HEAD_DIM
    kern = functools.partial(_chunk_attn_kernel, tq=tq, heads=heads, scale=1.0 / math.sqrt(HEAD_DIM))
    return pl.pallas_call(
        kern,
        grid=(batch, ng, nq),
        in_specs=[pl.BlockSpec((tq, width), lambda b, g, i: (b * nq + i, g)),
                  pl.BlockSpec((seq, width), lambda b, g, i: (b, ng + g)),
                  pl.BlockSpec((seq, width), lambda b, g, i: (b, 2 * ng + g)),
                  pl.BlockSpec((heads, tq, 3 * tq), lambda b, g, i: (g, 0, 0))],
        out_specs=pl.BlockSpec((tq, width), lambda b, g, i: (b * nq + i, g)),
        out_shape=jax.ShapeDtypeStruct((batch * seq, nh * HEAD_DIM), BF16),
        compiler_params=_cparams(3),
        name="chunk_attention",
    )(qkv, qkv, qkv, bias_tab)


def _suffix_matrix(tk):
    j = np.arange(tk)[:, None]
    s = np.arange(tk)[None, :]
    later = -(j > s).astype(np.float32)
    half = np.concatenate([later, -np.ones((tk, tk), np.float32)], axis=1)
    return jnp.asarray(np.concatenate([half, half], axis=0), BF16)


def _softplus(z):
    return jnp.maximum(z, 0.0) + jnp.log(1.0 + jnp.exp(-jnp.abs(z)))


def _stick_kernel(q_ref, k_ref, v_ref, u_ref, o_ref, acc_ref, run_ref, *, tq, tk, heads, scale):
    i = pl.program_id(2)
    u = u_ref[...]
    row = lax.broadcasted_iota(jnp.int32, (tq, tq), 0)
    col = lax.broadcasted_iota(jnp.int32, (tq, tq), 1)
    causal = col < row

    def span(h, start, diagonal):
        hs = slice(h * HEAD_DIM, (h + 1) * HEAD_DIM)
        q = q_ref[:, hs]
        k = k_ref[pl.ds(start, tq), hs]
        v = v_ref[pl.ds(start, tq), hs]
        z = lax.dot_general(q, k, (((1,), (1,)), ((), ())), preferred_element_type=F32) * scale
        sp = _softplus(z)
        neg_log_keep = jnp.where(causal, sp, 0.0) if diagonal else sp
        run = run_ref[h]
        later = [None, None]
        for half in (1, 0):
            s = neg_log_keep[:, half * tk:(half + 1) * tk]
            hi = s.astype(BF16)
            lo = (s - hi.astype(F32)).astype(BF16)
            sums = jnp.dot(jnp.concatenate([hi, lo], axis=1), u, preferred_element_type=F32)
            later[half] = sums[:, :tk] + run
            run = run + sums[:, tk:]
        run_ref[h] = run
        a = jnp.exp(z - sp + jnp.concatenate(later, axis=1))
        if diagonal:
            a = jnp.where(causal, a, 0.0)
        acc_ref[h] += jnp.dot(a.astype(BF16), v, preferred_element_type=F32)

    acc_ref[...] = jnp.zeros_like(acc_ref)
    run_ref[...] = jnp.zeros_like(run_ref)
    for h in range(heads):
        span(h, pl.multiple_of(i * tq, tq), True)

    def body(t, c):
        start = pl.multiple_of((i - 1 - t) * tq, tq)
        for h in range(heads):
            span(h, start, False)
        return c

    lax.fori_loop(0, i, body, 0)
    for h in range(heads):
        o_ref[:, h * HEAD_DIM:(h + 1) * HEAD_DIM] = acc_ref[h].astype(o_ref.dtype)


def _stick_attention(qkv, *, batch, seq, tq, heads, col_off):
    nq = seq // tq
    nh = N_HEADS_B
    tk = tq // 2
    width = heads * HEAD_DIM
    assert col_off % width == 0 and nh % heads == 0
    c0 = col_off // width
    ng = nh // heads
    kern = functools.partial(_stick_kernel, tq=tq, tk=tk, heads=heads, scale=1.0 / math.sqrt(HEAD_DIM))
    return pl.pallas_call(
        kern,
        grid=(batch, ng, nq),
        in_specs=[pl.BlockSpec((tq, width), lambda b, g, i: (b * nq + i, c0 + g)),
                  pl.BlockSpec((seq, width), lambda b, g, i: (b, c0 + ng + g)),
                  pl.BlockSpec((seq, width), lambda b, g, i: (b, c0 + 2 * ng + g)),
                  pl.BlockSpec((2 * tk, 2 * tk), lambda b, g, i: (0, 0))],
        out_specs=pl.BlockSpec((tq, width), lambda b, g, i: (b * nq + i, g)),
        out_shape=jax.ShapeDtypeStruct((batch * seq, nh * HEAD_DIM), BF16),
        scratch_shapes=[pltpu.VMEM((heads, tq, HEAD_DIM), F32), pltpu.VMEM((heads, tq, tk), F32)],
        compiler_params=_cparams(3),
        name="stick_attention",
    )(qkv, qkv, qkv, _suffix_matrix(tk))


def _router_kernel(h_ref, g_ref, wr_ref, br_ref, xn_ref, idx_ref, wt_ref, rank_ref, cnt_ref,
                   wr_hi, wr_lo, carry, *, tm):
    step = pl.program_id(0)

    @pl.when(step == 0)
    def _():
        w = wr_ref[...]
        hi = w.astype(BF16)
        wr_hi[...] = hi
        wr_lo[...] = (w - hi.astype(F32)).astype(BF16)
        carry[...] = jnp.zeros_like(carry)

    x = h_ref[...]
    ms = jnp.mean(x * x, axis=-1, keepdims=True)
    xn = x * lax.rsqrt(ms + EPS) * g_ref[...]
    xn_ref[...] = xn.reshape(xn_ref.shape).astype(xn_ref.dtype)
    xh = xn.astype(BF16)
    xl = (xn - xh.astype(F32)).astype(BF16)
    logits = (jnp.dot(xh, wr_hi[...], preferred_element_type=F32)
              + jnp.dot(xh, wr_lo[...], preferred_element_type=F32)
              + jnp.dot(xl, wr_hi[...], preferred_element_type=F32)) + br_ref[...]
    lane = lax.broadcasted_iota(jnp.int32, (tm, V7X_LANES), 1)
    logits = jnp.where(lane < N_EXPERTS, logits, -jnp.inf)

    r = lax.broadcasted_iota(jnp.int32, (tm, tm), 0)
    c = lax.broadcasted_iota(jnp.int32, (tm, tm), 1)
    tri = (c < r).astype(BF16)

    sel_mask = jnp.zeros((tm, V7X_LANES), F32)
    onehots, vals = [], []
    work = logits
    for _ in range(TOP_K):
        mx = jnp.max(work, axis=-1, keepdims=True)
        first = jnp.min(jnp.where(work == mx, lane, V7X_LANES), axis=-1, keepdims=True)
        oh = lane == first
        onehots.append(oh)
        vals.append(mx)
        sel_mask = sel_mask + oh.astype(F32)
        work = jnp.where(oh, -jnp.inf, work)

    prefix = jnp.dot(tri, sel_mask.astype(BF16), preferred_element_type=F32) + carry[...]
    carry[...] = carry[...] + jnp.sum(sel_mask, axis=0, keepdims=True)
    cnt_ref[...] = jnp.broadcast_to(carry[...], cnt_ref.shape)

    es = [jnp.exp(v - vals[0]) for v in vals]
    denom = es[0] + es[1] + es[2] + es[3]
    idx_out = jnp.zeros((tm, V7X_LANES), jnp.int32)
    wt_out = jnp.zeros((tm, V7X_LANES), F32)
    rank_out = jnp.zeros((tm, V7X_LANES), jnp.int32)
    for k in range(TOP_K):
        oh = onehots[k]
        e_k = jnp.sum(jnp.where(oh, lane, 0), axis=-1, keepdims=True)
        r_k = jnp.sum(jnp.where(oh, prefix, 0.0), axis=-1, keepdims=True).astype(jnp.int32)
        idx_out = jnp.where(lane == k, e_k, idx_out)
        wt_out = jnp.where(lane == k, es[k] / denom, wt_out)
        rank_out = jnp.where(lane == k, r_k, rank_out)
    idx_ref[...] = idx_out
    wt_ref[...] = wt_out
    rank_ref[...] = rank_out


def _router(h, g, w_router, b_router, *, tm):
    t, d = h.shape
    wr = jnp.zeros((d, V7X_LANES), F32).at[:, :N_EXPERTS].set(w_router)
    br = jnp.zeros((1, V7X_LANES), F32).at[0, :N_EXPERTS].set(b_router)
    row = lambda i: (i, 0)
    fixed = lambda i: (0, 0)
    kern = functools.partial(_router_kernel, tm=tm)
    return pl.pallas_call(
        kern,
        grid=(t // tm,),
        in_specs=[pl.BlockSpec((tm, d), row),
                  pl.BlockSpec((1, d), fixed),
                  pl.BlockSpec((d, V7X_LANES), fixed),
                  pl.BlockSpec((1, V7X_LANES), fixed)],
        out_specs=[pl.BlockSpec((tm, d // V7X_LANES, V7X_LANES), lambda i: (i, 0, 0)),
                   pl.BlockSpec((tm, V7X_LANES), row),
                   pl.BlockSpec((tm, V7X_LANES), row),
                   pl.BlockSpec((tm, V7X_LANES), row),
                   pl.BlockSpec((8, V7X_LANES), fixed)],
        out_shape=[jax.ShapeDtypeStruct((t, d // V7X_LANES, V7X_LANES), BF16),
                   jax.ShapeDtypeStruct((t, V7X_LANES), jnp.int32),
                   jax.ShapeDtypeStruct((t, V7X_LANES), F32),
                   jax.ShapeDtypeStruct((t, V7X_LANES), jnp.int32),
                   jax.ShapeDtypeStruct((8, V7X_LANES), F32)],
        scratch_shapes=[pltpu.VMEM((d, V7X_LANES), BF16),
                        pltpu.VMEM((d, V7X_LANES), BF16),
                        pltpu.VMEM((1, V7X_LANES), F32)],
        compiler_params=_cparams(1),
        name="router",
    )(h, g.reshape(1, d), wr, br)


def _row_copy(src_hbm, dst, sem, src_row, dst_row):
    return pltpu.make_async_copy(src_hbm.at[pl.ds(src_row, 1)], dst.at[pl.ds(dst_row, 1)], sem)


def _gather_kernel(dest_ref, nu_ref, x_hbm, o_ref, tok_ref, buf, sem, *, bm):
    i = pl.program_id(0)
    n_used = nu_ref[0]

    def issue(block):
        slot = block % 2

        def body(rp, c):
            for p in range(2):
                r = rp * 2 + p
                _row_copy(x_hbm, buf.at[slot], sem.at[slot], tok_ref[block * bm + r], r).start(priority=p)
            return c

        lax.fori_loop(0, bm // 2, body, 0, unroll=4)

    @pl.when(i == 0)
    def _():
        def clear(r, c):
            tok_ref[r] = 0
            return c

        lax.fori_loop(0, tok_ref.shape[0], clear, 0, unroll=8)

        def invert(n, c):
            tok_ref[dest_ref[n]] = lax.shift_right_logical(n, TOP_K_SHIFT)
            return c

        lax.fori_loop(0, dest_ref.shape[0], invert, 0, unroll=8)
        issue(i)

    @pl.when(i + 1 < n_used)
    def _():
        issue(i + 1)

    @pl.when(i < n_used)
    def _():
        slot = i % 2
        pltpu.make_async_copy(x_hbm.at[pl.ds(0, bm)], buf.at[slot], sem.at[slot]).wait()
        o_ref[...] = buf[slot].reshape(o_ref.shape)

    @pl.when(i >= n_used)
    def _():
        o_ref[...] = jnp.zeros_like(o_ref)


def _gather_rows(dest, n_used, xn, *, bm, n_rows):
    _, sub, lanes = xn.shape
    d = sub * lanes
    kern = functools.partial(_gather_kernel, bm=bm)
    return pl.pallas_call(
        kern,
        grid_spec=pltpu.PrefetchScalarGridSpec(
            num_scalar_prefetch=2,
            grid=(n_rows // bm,),
            in_specs=[pl.BlockSpec(memory_space=pl.ANY)],
            out_specs=pl.BlockSpec((bm, d), lambda i, dst, nu: (i, 0)),
            scratch_shapes=[pltpu.SMEM((n_rows,), jnp.int32),
                            pltpu.VMEM((2, bm, sub, lanes), xn.dtype), pltpu.SemaphoreType.DMA((2,))]),
        out_shape=jax.ShapeDtypeStruct((n_rows, d), xn.dtype),
        compiler_params=_cparams(1),
        name="gather_rows",
    )(dest, n_used, xn)


def _weight_ring_step(sched, nj, copies):
    be, _, first, group, next_expert, n_groups, _ = sched
    j, i = pl.program_id(0), pl.program_id(1)
    seq = j * n_groups[0] + group[i]
    slot = seq % 2

    @pl.when(first[i] == 1)
    def _():
        @pl.when(seq == 0)
        def _():
            for c in copies(be[i], j, slot):
                c.start(priority=1)

        last_group = group[i] == n_groups[0] - 1

        @pl.when(jnp.logical_not(jnp.logical_and(last_group, j == nj - 1)))
        def _():
            for c in copies(next_expert[i], j + last_group.astype(jnp.int32), 1 - slot):
                c.start(priority=1)

        for c in copies(be[i], j, slot):
            c.wait()

    return slot


N_SCHED = 7


def _for_valid_rows(valid, bm, body):
    half = bm // 2

    @pl.when(valid > half)
    def _():
        body(bm)

    @pl.when(valid <= half)
    def _():
        body(half)


def _expert_up_kernel(*refs, tf, d_ff, nj):
    sched, (xs_ref, w_hbm, bg_ref, bl_ref, h_ref, wbuf, sem) = refs[:N_SCHED], refs[N_SCHED:]
    n_used = sched[1]
    i = pl.program_id(1)

    def copies(e, j, slot):
        col = pl.multiple_of(j * tf, tf)
        return (pltpu.make_async_copy(w_hbm.at[e, :, pl.ds(col, tf)], wbuf.at[slot, 0], sem.at[slot]),
                pltpu.make_async_copy(w_hbm.at[e, :, pl.ds(d_ff + col, tf)], wbuf.at[slot, 1], sem.at[slot]))

    @pl.when(i < n_used[0])
    def _():
        slot = _weight_ring_step(sched, nj, copies)
        bm = xs_ref.shape[0]

        def body(rows):
            x = xs_ref[:rows]
            g = jnp.dot(x, wbuf[slot, 0].astype(BF16), preferred_element_type=F32) + bg_ref[0]
            l = jnp.dot(x, wbuf[slot, 1].astype(BF16), preferred_element_type=F32) + bl_ref[0]
            glu = jnp.minimum(g, SWIGLU_LIMIT)
            lin = jnp.clip(l, -SWIGLU_LIMIT, SWIGLU_LIMIT)
            h_ref[:rows] = (glu * jax.nn.sigmoid(SWIGLU_ALPHA * glu) * (lin + 1.0)).astype(h_ref.dtype)
            if rows < bm:
                h_ref[rows:] = jnp.zeros((bm - rows, h_ref.shape[1]), h_ref.dtype)

        _for_valid_rows(sched[6][i], bm, body)

    @pl.when(i >= n_used[0])
    def _():
        h_ref[...] = jnp.zeros_like(h_ref)


def _expert_up(sched, xs, w_gate_up, b_gate_up, *, bm, tf):
    n_rows, d = xs.shape
    n_e, _, two_ff = w_gate_up.shape
    d_ff = two_ff // 2
    nj = d_ff // tf

    def blk(i, s):
        return jnp.minimum(i, s[1][0] - 1)

    kern = functools.partial(_expert_up_kernel, tf=tf, d_ff=d_ff, nj=nj)
    return pl.pallas_call(
        kern,
        grid_spec=pltpu.PrefetchScalarGridSpec(
            num_scalar_prefetch=len(sched),
            grid=(nj, n_rows // bm),
            in_specs=[pl.BlockSpec((bm, d), lambda j, i, *s: (blk(i, s), 0)),
                      pl.BlockSpec(memory_space=pl.ANY),
                      pl.BlockSpec((1, 1, tf), lambda j, i, *s: (s[0][blk(i, s)], 0, j)),
                      pl.BlockSpec((1, 1, tf), lambda j, i, *s: (s[0][blk(i, s)], 0, nj + j))],
            out_specs=pl.BlockSpec((bm, tf), lambda j, i, *s: (i, j)),
            scratch_shapes=[pltpu.VMEM((2, 2, d, tf), F32), pltpu.SemaphoreType.DMA((2,))]),
        out_shape=jax.ShapeDtypeStruct((n_rows, d_ff), BF16),
        compiler_params=_cparams(2),
        name="expert_up",
    )(*sched, xs, w_gate_up, b_gate_up.reshape(n_e, 1, two_ff), b_gate_up.reshape(n_e, 1, two_ff))


def _expert_down_kernel(*refs, tn, nj):
    sched, (h_ref, w_hbm, b_ref, y_ref, wbuf, sem) = refs[:N_SCHED], refs[N_SCHED:]
    n_used = sched[1]
    i = pl.program_id(1)

    def copies(e, j, slot):
        col = pl.multiple_of(j * tn, tn)
        return (pltpu.make_async_copy(w_hbm.at[e, :, pl.ds(col, tn)], wbuf.at[slot], sem.at[slot]),)

    @pl.when(i < n_used[0])
    def _():
        slot = _weight_ring_step(sched, nj, copies)
        bm = h_ref.shape[0]

        def body(rows):
            y_ref[:rows] = (jnp.dot(h_ref[:rows], wbuf[slot].astype(BF16), preferred_element_type=F32)
                            + b_ref[0])
            if rows < bm:
                y_ref[rows:] = jnp.zeros((bm - rows, y_ref.shape[1]), y_ref.dtype)

        _for_valid_rows(sched[6][i], bm, body)

    @pl.when(i >= n_used[0])
    def _():
        y_ref[...] = jnp.zeros_like(y_ref)


def _expert_down(sched, hidden, w_down, b_down, *, bm, tn):
    n_rows, d_ff = hidden.shape
    n_e, _, d = w_down.shape
    nj = d // tn

    def blk(i, s):
        return jnp.minimum(i, s[1][0] - 1)

    kern = functools.partial(_expert_down_kernel, tn=tn, nj=nj)
    return pl.pallas_call(
        kern,
        grid_spec=pltpu.PrefetchScalarGridSpec(
            num_scalar_prefetch=len(sched),
            grid=(nj, n_rows // bm),
            in_specs=[pl.BlockSpec((bm, d_ff), lambda j, i, *s: (blk(i, s), 0)),
                      pl.BlockSpec(memory_space=pl.ANY),
                      pl.BlockSpec((1, 1, tn), lambda j, i, *s: (s[0][blk(i, s)], 0, j))],
            out_specs=pl.BlockSpec((bm, tn), lambda j, i, *s: (i, j)),
            scratch_shapes=[pltpu.VMEM((2, d_ff, tn), F32), pltpu.SemaphoreType.DMA((2,))]),
        out_shape=jax.ShapeDtypeStruct((n_rows, d), F32),
        compiler_params=_cparams(2),
        name="expert_down",
    )(*sched, hidden, w_down, b_down.reshape(n_e, 1, d))


def _combine_kernel(dest_ref, h_ref, wt_ref, ys_hbm, g_ref, o_ref, buf, sem, *, tm):
    i = pl.program_id(0)

    def issue(tile):
        slot = tile % 2

        def body(r, c):
            for k in range(TOP_K):
                _row_copy(ys_hbm, buf.at[slot, k], sem.at[slot],
                          dest_ref[(tile * tm + r) * TOP_K + k], r).start(priority=k % 2)
            return c

        lax.fori_loop(0, tm, body, 0, unroll=4)

    @pl.when(i == 0)
    def _():
        issue(i)

    @pl.when(i + 1 < pl.num_programs(0))
    def _():
        issue(i + 1)

    slot = i % 2
    for k in range(TOP_K):
        pltpu.make_async_copy(ys_hbm.at[pl.ds(0, tm)], buf.at[slot, k], sem.at[slot]).wait()
    wt = wt_ref[...]
    y = h_ref[...]
    for k in range(TOP_K):
        y = y + wt[:, k:k + 1] * buf[slot, k]
    ms = jnp.mean(y * y, axis=-1, keepdims=True)
    o_ref[...] = y * lax.rsqrt(ms + EPS) * g_ref[...]


def _combine(dest_flat, h, wts, ys, g, *, tm):
    t, d = h.shape
    kern = functools.partial(_combine_kernel, tm=tm)
    return pl.pallas_call(
        kern,
        grid_spec=pltpu.PrefetchScalarGridSpec(
            num_scalar_prefetch=1,
            grid=(t // tm,),
            in_specs=[pl.BlockSpec((tm, d), lambda i, dst: (i, 0)),
                      pl.BlockSpec((tm, V7X_LANES), lambda i, dst: (i, 0)),
                      pl.BlockSpec(memory_space=pl.ANY),
                      pl.BlockSpec((1, d), lambda i, dst: (0, 0))],
            out_specs=pl.BlockSpec((tm, d), lambda i, dst: (i, 0)),
            scratch_shapes=[pltpu.VMEM((2, TOP_K, tm) + ys.shape[1:], ys.dtype),
                            pltpu.SemaphoreType.DMA((2,))]),
        out_shape=jax.ShapeDtypeStruct((t, d), F32),
        compiler_params=_cparams(1),
        name="combine",
    )(dest_flat, h, wts, ys, g.reshape(1, d))


class _Tiles:
    norm_rows = 256
    mm_rows = 1024
    mm_cols = 512
    chunk_q = 256
    chunk_heads = 4
    stick_q = 256
    stick_heads = 8
    router_rows = 256
    moe_block = 256
    up_cols = 512
    down_cols = 2048
    combine_rows = 128


def _moe_layout(idx, rank, counts, *, bm):
    t = idx.shape[0]
    n_assign = t * TOP_K
    n_blocks = n_assign // bm + N_EXPERTS
    padded = (counts + bm - 1) // bm * bm
    padded_end = jnp.cumsum(padded)
    padded_start = padded_end - padded
    dest = (padded_start[idx] + rank).reshape(-1).astype(jnp.int32)
    blk_start = jnp.arange(n_blocks, dtype=jnp.int32) * bm
    block_expert = jnp.minimum(
        jnp.sum((padded_end[None, :] <= blk_start[:, None]).astype(jnp.int32), axis=1), N_EXPERTS - 1)
    n_used = (padded_end[-1:] // bm).astype(jnp.int32)
    has = padded > 0
    e_ids = jnp.arange(N_EXPERTS, dtype=jnp.int32)
    later = has[None, :] & (e_ids[None, :] > e_ids[:, None])
    first_e = jnp.min(jnp.where(has, e_ids, N_EXPERTS))
    next_later = jnp.min(jnp.where(later, e_ids[None, :], N_EXPERTS), axis=1)
    next_of_expert = jnp.where(next_later < N_EXPERTS, next_later, first_e)
    group_of_expert = jnp.cumsum(has.astype(jnp.int32)) - 1
    used = blk_start < padded_end[-1]
    first = (used & (blk_start == padded_start[block_expert])).astype(jnp.int32)
    real_end = padded_start + counts
    valid_rows = jnp.clip(real_end[block_expert] - blk_start, 0, bm) * used
    sched = (block_expert.astype(jnp.int32), n_used, first,
             group_of_expert[block_expert].astype(jnp.int32),
             next_of_expert[block_expert].astype(jnp.int32),
             jnp.sum(has.astype(jnp.int32)).reshape(1),
             valid_rows.astype(jnp.int32))
    assert len(sched) == N_SCHED
    return dest, sched


def kernel(x, norm_mix_g, w_in, b_gate, rel_bias, w_branch_a, w_branch_b, w_out, norm_ffn_g,
           w_router, b_router, w_gate_up, b_gate_up, w_down, b_down, norm_final_g):
    batch, seq, d = x.shape
    t = batch * seq
    tl = _Tiles
    width_a = N_HEADS_A * HEAD_DIM
    width_b = N_HEADS_B * HEAD_DIM
    qkv_width = 3 * width_a + 3 * width_b
    xt = x.reshape(t, d)

    xn = _rmsnorm(xt, norm_mix_g, tm=tl.norm_rows, out_dtype=BF16)
    qkv = _mm(xn, w_in, col_off=0, n_out=qkv_width, tm=tl.mm_rows, tn=tl.mm_cols,
              out_dtype=BF16, name="proj_qkv")
    gates = _mm(xn, w_in, col_off=qkv_width, n_out=2 * d, tm=tl.mm_rows, tn=tl.mm_cols,
                out_dtype=BF16, bias=b_gate, act="sigmoid", name="proj_gates")
    att_a = _chunk_attention(qkv, _chunk_bias_table(rel_bias, tl.chunk_q),
                             batch=batch, seq=seq, tq=tl.chunk_q, heads=tl.chunk_heads)
    att_b = _stick_attention(qkv, batch=batch, seq=seq, tq=tl.stick_q, heads=tl.stick_heads,
                             col_off=3 * width_a)
    merged = _branch_merge(att_a, att_b, w_branch_a, w_branch_b, gates, tm=tl.mm_rows, tn=tl.mm_cols)
    h = _mm(merged, w_out, col_off=0, n_out=d, tm=tl.mm_rows, tn=tl.mm_cols,
            out_dtype=F32, res=xt, name="out_proj")

    xn2, idx_l, wt_l, rank_l, cnt = _router(h, norm_ffn_g, w_router, b_router, tm=tl.router_rows)
    counts = cnt[0, :N_EXPERTS].astype(jnp.int32)
    dest, sched = _moe_layout(idx_l[:, :TOP_K], rank_l[:, :TOP_K], counts, bm=tl.moe_block)
    xs = _gather_rows(dest, sched[1], xn2, bm=tl.moe_block, n_rows=sched[0].shape[0] * tl.moe_block)
    hidden = _expert_up(sched, xs, w_gate_up, b_gate_up, bm=tl.moe_block, tf=tl.up_cols)
    ys = _expert_down(sched, hidden, w_down, b_down, bm=tl.moe_block, tn=tl.down_cols)
    y = _combine(dest, h, wt_l, ys, norm_final_g, tm=tl.combine_rows)
    return y.reshape(batch, seq, d)
```

```python
import functools
import math

import jax
import jax.numpy as jnp
import numpy as np
from jax import lax
from jax.experimental import pallas as pl
from jax.experimental.pallas import tpu as pltpu

F32 = jnp.float32
BF16 = jnp.bfloat16

HEAD_DIM = 128
N_HEADS_A = 16
N_HEADS_B = 16
CHUNK = 64
LEFT_CHUNKS = 8
REL_CLIP = 256
N_EXPERTS = 32
TOP_K = 4
TOP_K_SHIFT = 2
assert 1 << TOP_K_SHIFT == TOP_K
SWIGLU_ALPHA = 1.702
SWIGLU_LIMIT = 7.0
EPS = 1e-5
NEG = -1e30

V7X_LANES = 128
V7X_VMEM_BYTES = 64 * 1024 * 1024
VMEM_LIMIT = V7X_VMEM_BYTES - 4 * 1024 * 1024


def _cparams(n_axes):
    return pltpu.CompilerParams(
        dimension_semantics=("arbitrary",) * n_axes, vmem_limit_bytes=VMEM_LIMIT)


def _rmsnorm_kernel(x_ref, g_ref, o_ref):
    x = x_ref[...]
    ms = jnp.mean(x * x, axis=-1, keepdims=True)
    o_ref[...] = (x * lax.rsqrt(ms + EPS) * g_ref[...]).astype(o_ref.dtype)


def _rmsnorm(x, g, *, tm, out_dtype):
    t, d = x.shape
    return pl.pallas_call(
        _rmsnorm_kernel,
        grid=(t // tm,),
        in_specs=[pl.BlockSpec((tm, d), lambda i: (i, 0)),
                  pl.BlockSpec((1, d), lambda i: (0, 0))],
        out_specs=pl.BlockSpec((tm, d), lambda i: (i, 0)),
        out_shape=jax.ShapeDtypeStruct((t, d), out_dtype),
        compiler_params=_cparams(1),
        name="rmsnorm",
    )(x, g.reshape(1, d))


def _mm_kernel(*refs, has_bias, has_res, act):
    a_ref, w_ref = refs[0], refs[1]
    pos = 2
    b_ref = r_ref = None
    if has_bias:
        b_ref = refs[pos]
        pos += 1
    if has_res:
        r_ref = refs[pos]
        pos += 1
    o_ref = refs[pos]

    acc = jnp.dot(a_ref[...], w_ref[...].astype(BF16), preferred_element_type=F32)
    if has_bias:
        acc = acc + b_ref[...]
    if act == "sigmoid":
        acc = jax.nn.sigmoid(acc)
    if has_res:
        acc = acc + r_ref[...]
    o_ref[...] = acc.astype(o_ref.dtype)


def _mm(a, w, *, col_off, n_out, tm, tn, out_dtype, bias=None, res=None, act=None, name):
    m, k = a.shape
    assert w.shape[0] == k and col_off % tn == 0 and n_out % tn == 0 and m % tm == 0
    off = col_off // tn
    in_specs = [pl.BlockSpec((tm, k), lambda j, i: (i, 0)),
                pl.BlockSpec((k, tn), lambda j, i: (0, j + off))]
    args = [a, w]
    if bias is not None:
        in_specs.append(pl.BlockSpec((1, tn), lambda j, i: (0, j)))
        args.append(bias.reshape(1, n_out))
    if res is not None:
        in_specs.append(pl.BlockSpec((tm, tn), lambda j, i: (i, j)))
        args.append(res)
    kern = functools.partial(_mm_kernel, has_bias=bias is not None, has_res=res is not None, act=act)
    return pl.pallas_call(
        kern,
        grid=(n_out // tn, m // tm),
        in_specs=in_specs,
        out_specs=pl.BlockSpec((tm, tn), lambda j, i: (i, j)),
        out_shape=jax.ShapeDtypeStruct((m, n_out), out_dtype),
        compiler_params=_cparams(2),
        name=name,
    )(*args)


def _branch_kernel(a_ref, b_ref, wa_ref, wb_ref, ga_ref, gb_ref, o_ref):
    ya = jnp.dot(a_ref[...], wa_ref[...].astype(BF16), preferred_element_type=F32)
    yb = jnp.dot(b_ref[...], wb_ref[...].astype(BF16), preferred_element_type=F32)
    o_ref[...] = (ga_ref[...].astype(F32) * ya + gb_ref[...].astype(F32) * yb).astype(o_ref.dtype)


def _branch_merge(att_a, att_b, wa, wb, gates, *, tm, tn):
    m, ka = att_a.shape
    kb = att_b.shape[1]
    d = wa.shape[1]
    nj = d // tn
    return pl.pallas_call(
        _branch_kernel,
        grid=(nj, m // tm),
        in_specs=[pl.BlockSpec((tm, ka), lambda j, i: (i, 0)),
                  pl.BlockSpec((tm, kb), lambda j, i: (i, 0)),
                  pl.BlockSpec((ka, tn), lambda j, i: (0, j)),
                  pl.BlockSpec((kb, tn), lambda j, i: (0, j)),
                  pl.BlockSpec((tm, tn), lambda j, i: (i, j)),
                  pl.BlockSpec((tm, tn), lambda j, i: (i, j + nj))],
        out_specs=pl.BlockSpec((tm, tn), lambda j, i: (i, j)),
        out_shape=jax.ShapeDtypeStruct((m, d), BF16),
        compiler_params=_cparams(2),
        name="branch_merge",
    )(att_a, att_b, wa, wb, gates, gates)


def _chunk_bias_table(rel_bias, tq):
    left = LEFT_CHUNKS * CHUNK
    nk = 3 * tq
    assert left == 2 * tq
    qi = np.arange(tq)[:, None]
    kw = np.arange(nk)[None, :]
    krel = kw - left
    qc = qi // CHUNK
    kc = np.floor_divide(krel, CHUNK)
    valid = (kc >= qc - LEFT_CHUNKS) & (kc <= qc)
    rel_min, rel_max = left - nk + 1, left + tq - 1
    assert -REL_CLIP <= rel_min and rel_max > REL_CLIP
    rb = rel_bias.astype(F32).T
    nh = rb.shape[0]
    by_rel = jnp.concatenate(
        [rb[:, rel_min + REL_CLIP:], jnp.broadcast_to(rb[:, -1:], (nh, rel_max - REL_CLIP))], axis=1)
    period = rel_max - rel_min + 1
    w = jnp.concatenate([by_rel[:, :nk][:, ::-1], by_rel[:, nk:][:, ::-1]], axis=1)
    bias = jnp.tile(w, (1, tq))[:, :tq * (period - 1)].reshape(nh, tq, period - 1)[:, :, :nk]
    return jnp.where(jnp.asarray(valid)[None], bias, NEG)


def _chunk_attn_kernel(q_ref, k_ref, v_ref, bias_ref, o_ref, *, tq, heads, scale):
    i = pl.program_id(2)
    starts = [pl.multiple_of(jnp.maximum(i - 2 + j, 0) * tq, tq) for j in range(3)]
    for h in range(heads):
        hs = slice(h * HEAD_DIM, (h + 1) * HEAD_DIM)
        q = q_ref[:, hs]
        s_blocks = []
        for j in range(3):
            k = k_ref[pl.ds(starts[j], tq), hs]
            s = lax.dot_general(q, k, (((1,), (1,)), ((), ())), preferred_element_type=F32) * scale
            s = s + bias_ref[h, :, j * tq:(j + 1) * tq]
            if j < 2:
                s = jnp.where(i - 2 + j >= 0, s, NEG)
            s_blocks.append(s)
        m = jnp.maximum(jnp.maximum(jnp.max(s_blocks[0], axis=-1, keepdims=True),
                                    jnp.max(s_blocks[1], axis=-1, keepdims=True)),
                        jnp.max(s_blocks[2], axis=-1, keepdims=True))
        l = jnp.zeros_like(m)
        acc = jnp.zeros((tq, HEAD_DIM), F32)
        for j in range(3):
            p = jnp.exp(s_blocks[j] - m)
            l = l + jnp.sum(p, axis=-1, keepdims=True)
            v = v_ref[pl.ds(starts[j], tq), hs]
            acc = acc + jnp.dot(p.astype(BF16), v, preferred_element_type=F32)
        o_ref[:, hs] = (acc / l).astype(o_ref.dtype)


def _chunk_attention(qkv, bias_tab, *, batch, seq, tq, heads):
    nq = seq // tq
    nh = N_HEADS_A
    assert nh % heads == 0
    ng = nh // heads
    width = heads * HEAD_DIM
    kern = functools.partial(_chunk_attn_kernel, tq=tq, heads=heads, scale=1.0 / math.sqrt(HEAD_DIM))
    return pl.pallas_call(
        kern,
        grid=(batch, ng, nq),
        in_specs=[pl.BlockSpec((tq, width), lambda b, g, i: (b * nq + i, g)),
                  pl.BlockSpec((seq, width), lambda b, g, i: (b, ng + g)),
                  pl.BlockSpec((seq, width), lambda b, g, i: (b, 2 * ng + g)),
                  pl.BlockSpec((heads, tq, 3 * tq), lambda b, g, i: (g, 0, 0))],
        out_specs=pl.BlockSpec((tq, width), lambda b, g, i: (b * nq + i, g)),
        out_shape=jax.ShapeDtypeStruct((batch * seq, nh * HEAD_DIM), BF16),
        compiler_params=_cparams(3),
        name="chunk_attention",
    )(qkv, qkv, qkv, bias_tab)


def _suffix_matrix(tk):
    j = np.arange(tk)[:, None]
    s = np.arange(tk)[None, :]
    later = -(j > s).astype(np.float32)
    half = np.concatenate([later, -np.ones((tk, tk), np.float32)], axis=1)
    return jnp.asarray(np.concatenate([half, half], axis=0), BF16)


def _softplus(z):
    return jnp.maximum(z, 0.0) + jnp.log(1.0 + jnp.exp(-jnp.abs(z)))


def _stick_kernel(q_ref, k_ref, v_ref, u_ref, o_ref, acc_ref, run_ref, *, tq, tk, heads, scale):
    i = pl.program_id(2)
    u = u_ref[...]
    row = lax.broadcasted_iota(jnp.int32, (tq, tq), 0)
    col = lax.broadcasted_iota(jnp.int32, (tq, tq), 1)
    causal = col < row

    def span(h, start, diagonal):
        hs = slice(h * HEAD_DIM, (h + 1) * HEAD_DIM)
        q = q_ref[:, hs]
        k = k_ref[pl.ds(start, tq), hs]
        v = v_ref[pl.ds(start, tq), hs]
        z = lax.dot_general(q, k, (((1,), (1,)), ((), ())), preferred_element_type=F32) * scale
        sp = _softplus(z)
        neg_log_keep = jnp.where(causal, sp, 0.0) if diagonal else sp
        run = run_ref[h]
        later = [None, None]
        for half in (1, 0):
            s = neg_log_keep[:, half * tk:(half + 1) * tk]
            hi = s.astype(BF16)
            lo = (s - hi.astype(F32)).astype(BF16)
            sums = jnp.dot(jnp.concatenate([hi, lo], axis=1), u, preferred_element_type=F32)
            later[half] = sums[:, :tk] + run
            run = run + sums[:, tk:]
        run_ref[h] = run
        a = jnp.exp(z - sp + jnp.concatenate(later, axis=1))
        if diagonal:
            a = jnp.where(causal, a, 0.0)
        acc_ref[h] += jnp.dot(a.astype(BF16), v, preferred_element_type=F32)

    acc_ref[...] = jnp.zeros_like(acc_ref)
    run_ref[...] = jnp.zeros_like(run_ref)
    for h in range(heads):
        span(h, pl.multiple_of(i * tq, tq), True)

    def body(t, c):
        start = pl.multiple_of((i - 1 - t) * tq, tq)
        for h in range(heads):
            span(h, start, False)
        return c

    lax.fori_loop(0, i, body, 0)
    for h in range(heads):
        o_ref[:, h * HEAD_DIM:(h + 1) * HEAD_DIM] = acc_ref[h].astype(o_ref.dtype)


def _stick_attention(qkv, *, batch, seq, tq, heads, col_off):
    nq = seq // tq
    nh = N_HEADS_B
    tk = tq // 2
    width = heads * HEAD_DIM
    assert col_off % width == 0 and nh % heads == 0
    c0 = col_off // width
    ng = nh // heads
    kern = functools.partial(_stick_kernel, tq=tq, tk=tk, heads=heads, scale=1.0 / math.sqrt(HEAD_DIM))
    return pl.pallas_call(
        kern,
        grid=(batch, ng, nq),
        in_specs=[pl.BlockSpec((tq, width), lambda b, g, i: (b * nq + i, c0 + g)),
                  pl.BlockSpec((seq, width), lambda b, g, i: (b, c0 + ng + g)),
                  pl.BlockSpec((seq, width), lambda b, g, i: (b, c0 + 2 * ng + g)),
                  pl.BlockSpec((2 * tk, 2 * tk), lambda b, g, i: (0, 0))],
        out_specs=pl.BlockSpec((tq, width), lambda b, g, i: (b * nq + i, g)),
        out_shape=jax.ShapeDtypeStruct((batch * seq, nh * HEAD_DIM), BF16),
        scratch_shapes=[pltpu.VMEM((heads, tq, HEAD_DIM), F32), pltpu.VMEM((heads, tq, tk), F32)],
        compiler_params=_cparams(3),
        name="stick_attention",
    )(qkv, qkv, qkv, _suffix_matrix(tk))


def _router_kernel(h_ref, g_ref, wr_ref, br_ref, xn_ref, idx_ref, wt_ref, rank_ref, cnt_ref,
                   wr_split, carry, *, tm):
    step = pl.program_id(0)

    @pl.when(step == 0)
    def _():
        w = wr_ref[...]
        hi = w.astype(BF16)
        wr_split[:, :V7X_LANES] = hi
        wr_split[:, V7X_LANES:] = (w - hi.astype(F32)).astype(BF16)
        carry[...] = jnp.zeros_like(carry)

    x = h_ref[...]
    ms = jnp.mean(x * x, axis=-1, keepdims=True)
    xn = x * lax.rsqrt(ms + EPS) * g_ref[...]
    xn_ref[...] = xn.reshape(xn_ref.shape).astype(xn_ref.dtype)
    xh = xn.astype(BF16)
    xl = (xn - xh.astype(F32)).astype(BF16)
    both = jnp.dot(xh, wr_split[...], preferred_element_type=F32)
    logits = (both[:, :V7X_LANES] + both[:, V7X_LANES:]
              + jnp.dot(xl, wr_split[:, :V7X_LANES], preferred_element_type=F32)) + br_ref[...]
    lane = lax.broadcasted_iota(jnp.int32, (tm, V7X_LANES), 1)
    logits = jnp.where(lane < N_EXPERTS, logits, -jnp.inf)

    r = lax.broadcasted_iota(jnp.int32, (tm, tm), 0)
    c = lax.broadcasted_iota(jnp.int32, (tm, tm), 1)
    tri = (c < r).astype(BF16)

    sel_mask = jnp.zeros((tm, V7X_LANES), F32)
    onehots, vals = [], []
    work = logits
    for _ in range(TOP_K):
        mx = jnp.max(work, axis=-1, keepdims=True)
        first = jnp.min(jnp.where(work == mx, lane, V7X_LANES), axis=-1, keepdims=True)
        oh = lane == first
        onehots.append(oh)
        vals.append(mx)
        sel_mask = sel_mask + oh.astype(F32)
        work = jnp.where(oh, -jnp.inf, work)

    prefix = jnp.dot(tri, sel_mask.astype(BF16), preferred_element_type=F32) + carry[...]
    carry[...] = carry[...] + jnp.sum(sel_mask, axis=0, keepdims=True)
    cnt_ref[...] = jnp.broadcast_to(carry[...], cnt_ref.shape)

    es = [jnp.exp(v - vals[0]) for v in vals]
    denom = es[0] + es[1] + es[2] + es[3]
    idx_out = jnp.zeros((tm, V7X_LANES), jnp.int32)
    wt_out = jnp.zeros((tm, V7X_LANES), F32)
    rank_out = jnp.zeros((tm, V7X_LANES), jnp.int32)
    for k in range(TOP_K):
        oh = onehots[k]
        e_k = jnp.sum(jnp.where(oh, lane, 0), axis=-1, keepdims=True)
        r_k = jnp.sum(jnp.where(oh, prefix, 0.0), axis=-1, keepdims=True).astype(jnp.int32)
        idx_out = jnp.where(lane == k, e_k, idx_out)
        wt_out = jnp.where(lane == k, es[k] / denom, wt_out)
        rank_out = jnp.where(lane == k, r_k, rank_out)
    idx_ref[...] = idx_out
    wt_ref[...] = wt_out
    rank_ref[...] = rank_out


def _router(h, g, w_router, b_router, *, tm):
    t, d = h.shape
    wr = jnp.zeros((d, V7X_LANES), F32).at[:, :N_EXPERTS].set(w_router)
    br = jnp.zeros((1, V7X_LANES), F32).at[0, :N_EXPERTS].set(b_router)
    row = lambda i: (i, 0)
    fixed = lambda i: (0, 0)
    kern = functools.partial(_router_kernel, tm=tm)
    return pl.pallas_call(
        kern,
        grid=(t // tm,),
        in_specs=[pl.BlockSpec((tm, d), row),
                  pl.BlockSpec((1, d), fixed),
                  pl.BlockSpec((d, V7X_LANES), fixed),
                  pl.BlockSpec((1, V7X_LANES), fixed)],
        out_specs=[pl.BlockSpec((tm, d // V7X_LANES, V7X_LANES), lambda i: (i, 0, 0)),
                   pl.BlockSpec((tm, V7X_LANES), row),
                   pl.BlockSpec((tm, V7X_LANES), row),
                   pl.BlockSpec((tm, V7X_LANES), row),
                   pl.BlockSpec((8, V7X_LANES), fixed)],
        out_shape=[jax.ShapeDtypeStruct((t, d // V7X_LANES, V7X_LANES), BF16),
                   jax.ShapeDtypeStruct((t, V7X_LANES), jnp.int32),
                   jax.ShapeDtypeStruct((t, V7X_LANES), F32),
                   jax.ShapeDtypeStruct((t, V7X_LANES), jnp.int32),
                   jax.ShapeDtypeStruct((8, V7X_LANES), F32)],
        scratch_shapes=[pltpu.VMEM((d, 2 * V7X_LANES), BF16),
                        pltpu.VMEM((1, V7X_LANES), F32)],
        compiler_params=_cparams(1),
        name="router",
    )(h, g.reshape(1, d), wr, br)


def _row_copy(src_hbm, dst, sem, src_row, dst_row):
    return pltpu.make_async_copy(src_hbm.at[pl.ds(src_row, 1)], dst.at[pl.ds(dst_row, 1)], sem)


def _gather_kernel(dest_ref, nu_ref, x_hbm, o_ref, tok_ref, buf, sem, *, bm):
    i = pl.program_id(0)
    n_used = nu_ref[0]

    def issue(block):
        slot = block % 2

        def body(rp, c):
            for p in range(2):
                r = rp * 2 + p
                _row_copy(x_hbm, buf.at[slot], sem.at[slot], tok_ref[block * bm + r], r).start(priority=p)
            return c

        lax.fori_loop(0, bm // 2, body, 0, unroll=4)

    @pl.when(i == 0)
    def _():
        def clear(r, c):
            tok_ref[r] = 0
            return c

        lax.fori_loop(0, tok_ref.shape[0], clear, 0, unroll=8)

        def invert(n, c):
            tok_ref[dest_ref[n]] = lax.shift_right_logical(n, TOP_K_SHIFT)
            return c

        lax.fori_loop(0, dest_ref.shape[0], invert, 0, unroll=8)
        issue(i)

    @pl.when(i + 1 < n_used)
    def _():
        issue(i + 1)

    @pl.when(i < n_used)
    def _():
        slot = i % 2
        pltpu.make_async_copy(x_hbm.at[pl.ds(0, bm)], buf.at[slot], sem.at[slot]).wait()
        o_ref[...] = buf[slot].reshape(o_ref.shape)

    @pl.when(i >= n_used)
    def _():
        o_ref[...] = jnp.zeros_like(o_ref)


def _gather_rows(dest, n_used, xn, *, bm, n_rows):
    _, sub, lanes = xn.shape
    d = sub * lanes
    kern = functools.partial(_gather_kernel, bm=bm)
    return pl.pallas_call(
        kern,
        grid_spec=pltpu.PrefetchScalarGridSpec(
            num_scalar_prefetch=2,
            grid=(n_rows // bm,),
            in_specs=[pl.BlockSpec(memory_space=pl.ANY)],
            out_specs=pl.BlockSpec((bm, d), lambda i, dst, nu: (i, 0)),
            scratch_shapes=[pltpu.SMEM((n_rows,), jnp.int32),
                            pltpu.VMEM((2, bm, sub, lanes), xn.dtype), pltpu.SemaphoreType.DMA((2,))]),
        out_shape=jax.ShapeDtypeStruct((n_rows, d), xn.dtype),
        compiler_params=_cparams(1),
        name="gather_rows",
    )(dest, n_used, xn)


def _weight_ring_step(sched, nj, copies):
    be, _, first, group, next_expert, n_groups, _ = sched
    j, i = pl.program_id(0), pl.program_id(1)
    seq = j * n_groups[0] + group[i]
    slot = seq % 2

    @pl.when(first[i] == 1)
    def _():
        @pl.when(seq == 0)
        def _():
            for c in copies(be[i], j, slot):
                c.start(priority=1)

        last_group = group[i] == n_groups[0] - 1

        @pl.when(jnp.logical_not(jnp.logical_and(last_group, j == nj - 1)))
        def _():
            for c in copies(next_expert[i], j + last_group.astype(jnp.int32), 1 - slot):
                c.start(priority=1)

        for c in copies(be[i], j, slot):
            c.wait()

    return slot


N_SCHED = 7


def _for_valid_rows(valid, bm, body):
    half = bm // 2

    @pl.when(valid > half)
    def _():
        body(bm)

    @pl.when(valid <= half)
    def _():
        body(half)


def _expert_up_kernel(*refs, tf, d_ff, nj):
    sched, (xs_ref, w_hbm, bg_ref, bl_ref, h_ref, wbuf, sem) = refs[:N_SCHED], refs[N_SCHED:]
    n_used = sched[1]
    i = pl.program_id(1)

    def copies(e, j, slot):
        col = pl.multiple_of(j * tf, tf)
        return (pltpu.make_async_copy(w_hbm.at[e, :, pl.ds(col, tf)], wbuf.at[slot, 0], sem.at[slot]),
                pltpu.make_async_copy(w_hbm.at[e, :, pl.ds(d_ff + col, tf)], wbuf.at[slot, 1], sem.at[slot]))

    @pl.when(i < n_used[0])
    def _():
        slot = _weight_ring_step(sched, nj, copies)
        bm = xs_ref.shape[0]

        def body(rows):
            x = xs_ref[:rows]
            g = jnp.dot(x, wbuf[slot, 0].astype(BF16), preferred_element_type=F32) + bg_ref[0]
            l = jnp.dot(x, wbuf[slot, 1].astype(BF16), preferred_element_type=F32) + bl_ref[0]
            glu = jnp.minimum(g, SWIGLU_LIMIT)
            lin = jnp.clip(l, -SWIGLU_LIMIT, SWIGLU_LIMIT)
            h_ref[:rows] = (glu * jax.nn.sigmoid(SWIGLU_ALPHA * glu) * (lin + 1.0)).astype(h_ref.dtype)
            if rows < bm:
                h_ref[rows:] = jnp.zeros((bm - rows, h_ref.shape[1]), h_ref.dtype)

        _for_valid_rows(sched[6][i], bm, body)

    @pl.when(i >= n_used[0])
    def _():
        h_ref[...] = jnp.zeros_like(h_ref)


def _expert_up(sched, xs, w_gate_up, b_gate_up, *, bm, tf):
    n_rows, d = xs.shape
    n_e, _, two_ff = w_gate_up.shape
    d_ff = two_ff // 2
    nj = d_ff // tf

    def blk(i, s):
        return jnp.minimum(i, s[1][0] - 1)

    kern = functools.partial(_expert_up_kernel, tf=tf, d_ff=d_ff, nj=nj)
    return pl.pallas_call(
        kern,
        grid_spec=pltpu.PrefetchScalarGridSpec(
            num_scalar_prefetch=len(sched),
            grid=(nj, n_rows // bm),
            in_specs=[pl.BlockSpec((bm, d), lambda j, i, *s: (blk(i, s), 0)),
                      pl.BlockSpec(memory_space=pl.ANY),
                      pl.BlockSpec((1, 1, tf), lambda j, i, *s: (s[0][blk(i, s)], 0, j)),
                      pl.BlockSpec((1, 1, tf), lambda j, i, *s: (s[0][blk(i, s)], 0, nj + j))],
            out_specs=pl.BlockSpec((bm, tf), lambda j, i, *s: (i, j)),
            scratch_shapes=[pltpu.VMEM((2, 2, d, tf), F32), pltpu.SemaphoreType.DMA((2,))]),
        out_shape=jax.ShapeDtypeStruct((n_rows, d_ff), BF16),
        compiler_params=_cparams(2),
        name="expert_up",
    )(*sched, xs, w_gate_up, b_gate_up.reshape(n_e, 1, two_ff), b_gate_up.reshape(n_e, 1, two_ff))


def _expert_down_kernel(*refs, tn, nj):
    sched, (h_ref, w_hbm, b_ref, y_ref, wbuf, sem) = refs[:N_SCHED], refs[N_SCHED:]
    n_used = sched[1]
    i = pl.program_id(1)

    def copies(e, j, slot):
        col = pl.multiple_of(j * tn, tn)
        return (pltpu.make_async_copy(w_hbm.at[e, :, pl.ds(col, tn)], wbuf.at[slot], sem.at[slot]),)

    @pl.when(i < n_used[0])
    def _():
        slot = _weight_ring_step(sched, nj, copies)
        bm = h_ref.shape[0]

        def body(rows):
            y_ref[:rows] = (jnp.dot(h_ref[:rows], wbuf[slot].astype(BF16), preferred_element_type=F32)
                            + b_ref[0])
            if rows < bm:
                y_ref[rows:] = jnp.zeros((bm - rows, y_ref.shape[1]), y_ref.dtype)

        _for_valid_rows(sched[6][i], bm, body)

    @pl.when(i >= n_used[0])
    def _():
        y_ref[...] = jnp.zeros_like(y_ref)


def _expert_down(sched, hidden, w_down, b_down, *, bm, tn):
    n_rows, d_ff = hidden.shape
    n_e, _, d = w_down.shape
    nj = d // tn

    def blk(i, s):
        return jnp.minimum(i, s[1][0] - 1)

    kern = functools.partial(_expert_down_kernel, tn=tn, nj=nj)
    return pl.pallas_call(
        kern,
        grid_spec=pltpu.PrefetchScalarGridSpec(
            num_scalar_prefetch=len(sched),
            grid=(nj, n_rows // bm),
            in_specs=[pl.BlockSpec((bm, d_ff), lambda j, i, *s: (blk(i, s), 0)),
                      pl.BlockSpec(memory_space=pl.ANY),
                      pl.BlockSpec((1, 1, tn), lambda j, i, *s: (s[0][blk(i, s)], 0, j))],
            out_specs=pl.BlockSpec((bm, tn), lambda j, i, *s: (i, j)),
            scratch_shapes=[pltpu.VMEM((2, d_ff, tn), F32), pltpu.SemaphoreType.DMA((2,))]),
        out_shape=jax.ShapeDtypeStruct((n_rows, d), F32),
        compiler_params=_cparams(2),
        name="expert_down",
    )(*sched, hidden, w_down, b_down.reshape(n_e, 1, d))


def _combine_kernel(dest_ref, h_ref, wt_ref, ys_hbm, g_ref, o_ref, buf, sem, *, tm):
    i = pl.program_id(0)

    def issue(tile):
        slot = tile % 2

        def body(r, c):
            for k in range(TOP_K):
                _row_copy(ys_hbm, buf.at[slot, k], sem.at[slot],
                          dest_ref[(tile * tm + r) * TOP_K + k], r).start(priority=k % 2)
            return c

        lax.fori_loop(0, tm, body, 0, unroll=4)

    @pl.when(i == 0)
    def _():
        issue(i)

    @pl.when(i + 1 < pl.num_programs(0))
    def _():
        issue(i + 1)

    slot = i % 2
    for k in range(TOP_K):
        pltpu.make_async_copy(ys_hbm.at[pl.ds(0, tm)], buf.at[slot, k], sem.at[slot]).wait()
    wt = wt_ref[...]
    y = h_ref[...]
    for k in range(TOP_K):
        y = y + wt[:, k:k + 1] * buf[slot, k]
    ms = jnp.mean(y * y, axis=-1, keepdims=True)
    o_ref[...] = y * lax.rsqrt(ms + EPS) * g_ref[...]


def _combine(dest_flat, h, wts, ys, g, *, tm):
    t, d = h.shape
    kern = functools.partial(_combine_kernel, tm=tm)
    return pl.pallas_call(
        kern,
        grid_spec=pltpu.PrefetchScalarGridSpec(
            num_scalar_prefetch=1,
            grid=(t // tm,),
            in_specs=[pl.BlockSpec((tm, d), lambda i, dst: (i, 0)),
                      pl.BlockSpec((tm, V7X_LANES), lambda i, dst: (i, 0)),
                      pl.BlockSpec(memory_space=pl.ANY),
                      pl.BlockSpec((1, d), lambda i, dst: (0, 0))],
            out_specs=pl.BlockSpec((tm, d), lambda i, dst: (i, 0)),
            scratch_shapes=[pltpu.VMEM((2, TOP_K, tm) + ys.shape[1:], ys.dtype),
                            pltpu.SemaphoreType.DMA((2,))]),
        out_shape=jax.ShapeDtypeStruct((t, d), F32),
        compiler_params=_cparams(1),
        name="combine",
    )(dest_flat, h, wts, ys, g.reshape(1, d))


class _Tiles:
    norm_rows = 256
    mm_rows = 1024
    qkv_cols = 1024
    mm_cols = 512
    chunk_q = 256
    chunk_heads = 8
    stick_q = 256
    stick_heads = 8
    router_rows = 256
    moe_block = 256
    up_cols = 512
    down_cols = 2048
    combine_rows = 128


def _moe_layout(idx, rank, counts, *, bm):
    t = idx.shape[0]
    n_assign = t * TOP_K
    n_blocks = n_assign // bm + N_EXPERTS
    padded = (counts + bm - 1) // bm * bm
    padded_end = jnp.cumsum(padded)
    padded_start = padded_end - padded
    dest = (padded_start[idx] + rank).reshape(-1).astype(jnp.int32)
    blk_start = jnp.arange(n_blocks, dtype=jnp.int32) * bm
    block_expert = jnp.minimum(
        jnp.sum((padded_end[None, :] <= blk_start[:, None]).astype(jnp.int32), axis=1), N_EXPERTS - 1)
    n_used = (padded_end[-1:] // bm).astype(jnp.int32)
    has = padded > 0
    e_ids = jnp.arange(N_EXPERTS, dtype=jnp.int32)
    later = has[None, :] & (e_ids[None, :] > e_ids[:, None])
    first_e = jnp.min(jnp.where(has, e_ids, N_EXPERTS))
    next_later = jnp.min(jnp.where(later, e_ids[None, :], N_EXPERTS), axis=1)
    next_of_expert = jnp.where(next_later < N_EXPERTS, next_later, first_e)
    group_of_expert = jnp.cumsum(has.astype(jnp.int32)) - 1
    used = blk_start < padded_end[-1]
    first = (used & (blk_start == padded_start[block_expert])).astype(jnp.int32)
    real_end = padded_start + counts
    valid_rows = jnp.clip(real_end[block_expert] - blk_start, 0, bm) * used
    sched = (block_expert.astype(jnp.int32), n_used, first,
             group_of_expert[block_expert].astype(jnp.int32),
             next_of_expert[block_expert].astype(jnp.int32),
             jnp.sum(has.astype(jnp.int32)).reshape(1),
             valid_rows.astype(jnp.int32))
    assert len(sched) == N_SCHED
    return dest, sched


def kernel(x, norm_mix_g, w_in, b_gate, rel_bias, w_branch_a, w_branch_b, w_out, norm_ffn_g,
           w_router, b_router, w_gate_up, b_gate_up, w_down, b_down, norm_final_g):
    batch, seq, d = x.shape
    t = batch * seq
    tl = _Tiles
    width_a = N_HEADS_A * HEAD_DIM
    width_b = N_HEADS_B * HEAD_DIM
    qkv_width = 3 * width_a + 3 * width_b
    xt = x.reshape(t, d)

    xn = _rmsnorm(xt, norm_mix_g, tm=tl.norm_rows, out_dtype=BF16)
    qkv = _mm(xn, w_in, col_off=0, n_out=qkv_width, tm=tl.mm_rows, tn=tl.qkv_cols,
              out_dtype=BF16, name="proj_qkv")
    gates = _mm(xn, w_in, col_off=qkv_width, n_out=2 * d, tm=tl.mm_rows, tn=tl.mm_cols,
                out_dtype=BF16, bias=b_gate, act="sigmoid", name="proj_gates")
    att_a = _chunk_attention(qkv, _chunk_bias_table(rel_bias, tl.chunk_q),
                             batch=batch, seq=seq, tq=tl.chunk_q, heads=tl.chunk_heads)
    att_b = _stick_attention(qkv, batch=batch, seq=seq, tq=tl.stick_q, heads=tl.stick_heads,
                             col_off=3 * width_a)
    merged = _branch_merge(att_a, att_b, w_branch_a, w_branch_b, gates, tm=tl.mm_rows, tn=tl.mm_cols)
    h = _mm(merged, w_out, col_off=0, n_out=d, tm=tl.mm_rows, tn=tl.mm_cols,
            out_dtype=F32, res=xt, name="out_proj")

    xn2, idx_l, wt_l, rank_l, cnt = _router(h, norm_ffn_g, w_router, b_router, tm=tl.router_rows)
    counts = cnt[0, :N_EXPERTS].astype(jnp.int32)
    dest, sched = _moe_layout(idx_l[:, :TOP_K], rank_l[:, :TOP_K], counts, bm=tl.moe_block)
    xs = _gather_rows(dest, sched[1], xn2, bm=tl.moe_block, n_rows=sched[0].shape[0] * tl.moe_block)
    hidden = _expert_up(sched, xs, w_gate_up, b_gate_up, bm=tl.moe_block, tf=tl.up_cols)
    ys = _expert_down(sched, hidden, w_down, b_down, bm=tl.moe_block, tn=tl.down_cols)
    y = _combine(dest, h, wt_l, ys, norm_final_g, tm=tl.combine_rows)
    return y.reshape(batch, seq, d)
```

```python
import functools
import math

import jax
import jax.numpy as jnp
import numpy as np
from jax import lax
from jax.experimental import pallas as pl
from jax.experimental.pallas import tpu as pltpu

F32 = jnp.float32
BF16 = jnp.bfloat16

HEAD_DIM = 128
N_HEADS_A = 16
N_HEADS_B = 16
CHUNK = 64
LEFT_CHUNKS = 8
REL_CLIP = 256
N_EXPERTS = 32
TOP_K = 4
TOP_K_SHIFT = 2
assert 1 << TOP_K_SHIFT == TOP_K
SWIGLU_ALPHA = 1.702
SWIGLU_LIMIT = 7.0
EPS = 1e-5
NEG = -1e30

V7X_LANES = 128
V7X_VMEM_BYTES = 64 * 1024 * 1024
VMEM_LIMIT = V7X_VMEM_BYTES - 4 * 1024 * 1024


def _cparams(n_axes):
    return pltpu.CompilerParams(
        dimension_semantics=("arbitrary",) * n_axes, vmem_limit_bytes=VMEM_LIMIT)


def _rmsnorm_kernel(x_ref, g_ref, o_ref):
    x = x_ref[...]
    ms = jnp.mean(x * x, axis=-1, keepdims=True)
    o_ref[...] = (x * lax.rsqrt(ms + EPS) * g_ref[...]).astype(o_ref.dtype)


def _rmsnorm(x, g, *, tm, out_dtype):
    t, d = x.shape
    return pl.pallas_call(
        _rmsnorm_kernel,
        grid=(t // tm,),
        in_specs=[pl.BlockSpec((tm, d), lambda i: (i, 0)),
                  pl.BlockSpec((1, d), lambda i: (0, 0))],
        out_specs=pl.BlockSpec((tm, d), lambda i: (i, 0)),
        out_shape=jax.ShapeDtypeStruct((t, d), out_dtype),
        compiler_params=_cparams(1),
        name="rmsnorm",
    )(x, g.reshape(1, d))


def _mm_kernel(*refs, has_bias, has_res, act):
    a_ref, w_ref = refs[0], refs[1]
    pos = 2
    b_ref = r_ref = None
    if has_bias:
        b_ref = refs[pos]
        pos += 1
    if has_res:
        r_ref = refs[pos]
        pos += 1
    o_ref = refs[pos]

    acc = jnp.dot(a_ref[...], w_ref[...].astype(BF16), preferred_element_type=F32)
    if has_bias:
        acc = acc + b_ref[...]
    if act == "sigmoid":
        acc = jax.nn.sigmoid(acc)
    if has_res:
        acc = acc + r_ref[...]
    o_ref[...] = acc.astype(o_ref.dtype)


def _mm(a, w, *, col_off, n_out, tm, tn, out_dtype, bias=None, res=None, act=None, name):
    m, k = a.shape
    assert w.shape[0] == k and col_off % tn == 0 and n_out % tn == 0 and m % tm == 0
    off = col_off // tn
    in_specs = [pl.BlockSpec((tm, k), lambda j, i: (i, 0)),
                pl.BlockSpec((k, tn), lambda j, i: (0, j + off))]
    args = [a, w]
    if bias is not None:
        in_specs.append(pl.BlockSpec((1, tn), lambda j, i: (0, j)))
        args.append(bias.reshape(1, n_out))
    if res is not None:
        in_specs.append(pl.BlockSpec((tm, tn), lambda j, i: (i, j)))
        args.append(res)
    kern = functools.partial(_mm_kernel, has_bias=bias is not None, has_res=res is not None, act=act)
    return pl.pallas_call(
        kern,
        grid=(n_out // tn, m // tm),
        in_specs=in_specs,
        out_specs=pl.BlockSpec((tm, tn), lambda j, i: (i, j)),
        out_shape=jax.ShapeDtypeStruct((m, n_out), out_dtype),
        compiler_params=_cparams(2),
        name=name,
    )(*args)


def _branch_kernel(a_ref, b_ref, wa_ref, wb_ref, ga_ref, gb_ref, o_ref):
    ya = jnp.dot(a_ref[...], wa_ref[...].astype(BF16), preferred_element_type=F32)
    yb = jnp.dot(b_ref[...], wb_ref[...].astype(BF16), preferred_element_type=F32)
    o_ref[...] = (ga_ref[...].astype(F32) * ya + gb_ref[...].astype(F32) * yb).astype(o_ref.dtype)


def _branch_merge(att_a, att_b, wa, wb, gates, *, tm, tn):
    m, ka = att_a.shape
    kb = att_b.shape[1]
    d = wa.shape[1]
    nj = d // tn
    return pl.pallas_call(
        _branch_kernel,
        grid=(nj, m // tm),
        in_specs=[pl.BlockSpec((tm, ka), lambda j, i: (i, 0)),
                  pl.BlockSpec((tm, kb), lambda j, i: (i, 0)),
                  pl.BlockSpec((ka, tn), lambda j, i: (0, j)),
                  pl.BlockSpec((kb, tn), lambda j, i: (0, j)),
                  pl.BlockSpec((tm, tn), lambda j, i: (i, j)),
                  pl.BlockSpec((tm, tn), lambda j, i: (i, j + nj))],
        out_specs=pl.BlockSpec((tm, tn), lambda j, i: (i, j)),
        out_shape=jax.ShapeDtypeStruct((m, d), BF16),
        compiler_params=_cparams(2),
        name="branch_merge",
    )(att_a, att_b, wa, wb, gates, gates)


def _chunk_bias_table(rel_bias, tq):
    left = LEFT_CHUNKS * CHUNK
    nk = 3 * tq
    assert left == 2 * tq
    qi = np.arange(tq)[:, None]
    kw = np.arange(nk)[None, :]
    krel = kw - left
    qc = qi // CHUNK
    kc = np.floor_divide(krel, CHUNK)
    valid = (kc >= qc - LEFT_CHUNKS) & (kc <= qc)
    rel_min, rel_max = left - nk + 1, left + tq - 1
    assert -REL_CLIP <= rel_min and rel_max > REL_CLIP
    rb = rel_bias.astype(F32).T
    nh = rb.shape[0]
    by_rel = jnp.concatenate(
        [rb[:, rel_min + REL_CLIP:], jnp.broadcast_to(rb[:, -1:], (nh, rel_max - REL_CLIP))], axis=1)
    period = rel_max - rel_min + 1
    w = jnp.concatenate([by_rel[:, :nk][:, ::-1], by_rel[:, nk:][:, ::-1]], axis=1)
    bias = jnp.tile(w, (1, tq))[:, :tq * (period - 1)].reshape(nh, tq, period - 1)[:, :, :nk]
    return jnp.where(jnp.asarray(valid)[None], bias, NEG)


def _chunk_attn_kernel(q_ref, k_ref, v_ref, bias_ref, o_ref, *, tq, heads, scale):
    i = pl.program_id(2)
    starts = [pl.multiple_of(jnp.maximum(i - 2 + j, 0) * tq, tq) for j in range(3)]
    for h in range(heads):
        hs = slice(h * HEAD_DIM, (h + 1) * HEAD_DIM)
        q = q_ref[:, hs]
        s_blocks = []
        for j in range(3):
            k = k_ref[pl.ds(starts[j], tq), hs]
            s = lax.dot_general(q, k, (((1,), (1,)), ((), ())), preferred_element_type=F32) * scale
            s = s + bias_ref[h, :, j * tq:(j + 1) * tq]
            if j < 2:
                s = jnp.where(i - 2 + j >= 0, s, NEG)
            s_blocks.append(s)
        m = jnp.maximum(jnp.maximum(jnp.max(s_blocks[0], axis=-1, keepdims=True),
                                    jnp.max(s_blocks[1], axis=-1, keepdims=True)),
                        jnp.max(s_blocks[2], axis=-1, keepdims=True))
        l = jnp.zeros_like(m)
        acc = jnp.zeros((tq, HEAD_DIM), F32)
        for j in range(3):
            p = jnp.exp(s_blocks[j] - m)
            l = l + jnp.sum(p, axis=-1, keepdims=True)
            v = v_ref[pl.ds(starts[j], tq), hs]
            acc = acc + jnp.dot(p.astype(BF16), v, preferred_element_type=F32)
        o_ref[:, hs] = (acc / l).astype(o_ref.dtype)


def _chunk_attention(qkv, bias_tab, *, batch, seq, tq, heads):
    nq = seq // tq
    nh = N_HEADS_A
    assert nh % heads == 0
    ng = nh // heads
    width = heads * HEAD_DIM
    kern = functools.partial(_chunk_attn_kernel, tq=tq, heads=heads, scale=1.0 / math.sqrt(HEAD_DIM))
    return pl.pallas_call(
        kern,
        grid=(batch, ng, nq),
        in_specs=[pl.BlockSpec((tq, width), lambda b, g, i: (b * nq + i, g)),
                  pl.BlockSpec((seq, width), lambda b, g, i: (b, ng + g)),
                  pl.BlockSpec((seq, width), lambda b, g, i: (b, 2 * ng + g)),
                  pl.BlockSpec((heads, tq, 3 * tq), lambda b, g, i: (g, 0, 0))],
        out_specs=pl.BlockSpec((tq, width), lambda b, g, i: (b * nq + i, g)),
        out_shape=jax.ShapeDtypeStruct((batch * seq, nh * HEAD_DIM), BF16),
        compiler_params=_cparams(3),
        name="chunk_attention",
    )(qkv, qkv, qkv, bias_tab)


def _suffix_matrix(tk):
    j = np.arange(tk)[:, None]
    s = np.arange(tk)[None, :]
    later = -(j > s).astype(np.float32)
    half = np.concatenate([later, -np.ones((tk, tk), np.float32)], axis=1)
    return jnp.asarray(np.concatenate([half, half], axis=0), BF16)


def _softplus(z):
    return jnp.maximum(z, 0.0) + jnp.log(1.0 + jnp.exp(-jnp.abs(z)))


def _stick_kernel(q_ref, k_ref, v_ref, u_ref, o_ref, acc_ref, run_ref, *, tq, tk, heads, scale):
    i = pl.program_id(2)
    u = u_ref[...]
    row = lax.broadcasted_iota(jnp.int32, (tq, tq), 0)
    col = lax.broadcasted_iota(jnp.int32, (tq, tq), 1)
    causal = col < row

    def span(h, start, diagonal):
        hs = slice(h * HEAD_DIM, (h + 1) * HEAD_DIM)
        q = q_ref[:, hs]
        k = k_ref[pl.ds(start, tq), hs]
        v = v_ref[pl.ds(start, tq), hs]
        z = lax.dot_general(q, k, (((1,), (1,)), ((), ())), preferred_element_type=F32) * scale
        sp = _softplus(z)
        neg_log_keep = jnp.where(causal, sp, 0.0) if diagonal else sp
        run = run_ref[h]
        later = [None, None]
        for half in (1, 0):
            s = neg_log_keep[:, half * tk:(half + 1) * tk]
            hi = s.astype(BF16)
            lo = (s - hi.astype(F32)).astype(BF16)
            sums = jnp.dot(jnp.concatenate([hi, lo], axis=1), u, preferred_element_type=F32)
            later[half] = sums[:, :tk] + run
            run = run + sums[:, tk:]
        run_ref[h] = run
        a = jnp.exp(z - sp + jnp.concatenate(later, axis=1))
        if diagonal:
            a = jnp.where(causal, a, 0.0)
        acc_ref[h] += jnp.dot(a.astype(BF16), v, preferred_element_type=F32)

    acc_ref[...] = jnp.zeros_like(acc_ref)
    run_ref[...] = jnp.zeros_like(run_ref)
    for h in range(heads):
        span(h, pl.multiple_of(i * tq, tq), True)

    def body(t, c):
        start = pl.multiple_of((i - 1 - t) * tq, tq)
        for h in range(heads):
            span(h, start, False)
        return c

    lax.fori_loop(0, i, body, 0)
    for h in range(heads):
        o_ref[:, h * HEAD_DIM:(h + 1) * HEAD_DIM] = acc_ref[h].astype(o_ref.dtype)


def _stick_attention(qkv, *, batch, seq, tq, heads, col_off):
    nq = seq // tq
    nh = N_HEADS_B
    tk = tq // 2
    width = heads * HEAD_DIM
    assert col_off % width == 0 and nh % heads == 0
    c0 = col_off // width
    ng = nh // heads
    kern = functools.partial(_stick_kernel, tq=tq, tk=tk, heads=heads, scale=1.0 / math.sqrt(HEAD_DIM))
    return pl.pallas_call(
        kern,
        grid=(batch, ng, nq),
        in_specs=[pl.BlockSpec((tq, width), lambda b, g, i: (b * nq + i, c0 + g)),
                  pl.BlockSpec((seq, width), lambda b, g, i: (b, c0 + ng + g)),
                  pl.BlockSpec((seq, width), lambda b, g, i: (b, c0 + 2 * ng + g)),
                  pl.BlockSpec((2 * tk, 2 * tk), lambda b, g, i: (0, 0))],
        out_specs=pl.BlockSpec((tq, width), lambda b, g, i: (b * nq + i, g)),
        out_shape=jax.ShapeDtypeStruct((batch * seq, nh * HEAD_DIM), BF16),
        scratch_shapes=[pltpu.VMEM((heads, tq, HEAD_DIM), F32), pltpu.VMEM((heads, tq, tk), F32)],
        compiler_params=_cparams(3),
        name="stick_attention",
    )(qkv, qkv, qkv, _suffix_matrix(tk))


def _router_kernel(h_ref, g_ref, wr_ref, br_ref, xn_ref, idx_ref, wt_ref, rank_ref, cnt_ref,
                   wr_split, carry, *, tm):
    step = pl.program_id(0)

    @pl.when(step == 0)
    def _():
        w = wr_ref[...]
        hi = w.astype(BF16)
        wr_split[:, :V7X_LANES] = hi
        wr_split[:, V7X_LANES:] = (w - hi.astype(F32)).astype(BF16)
        carry[...] = jnp.zeros_like(carry)

    x = h_ref[...]
    ms = jnp.mean(x * x, axis=-1, keepdims=True)
    xn = x * lax.rsqrt(ms + EPS) * g_ref[...]
    xn_ref[...] = xn.reshape(xn_ref.shape).astype(xn_ref.dtype)
    xh = xn.astype(BF16)
    xl = (xn - xh.astype(F32)).astype(BF16)
    both = jnp.dot(xh, wr_split[...], preferred_element_type=F32)
    logits = (both[:, :V7X_LANES] + both[:, V7X_LANES:]
              + jnp.dot(xl, wr_split[:, :V7X_LANES], preferred_element_type=F32)) + br_ref[...]
    lane = lax.broadcasted_iota(jnp.int32, (tm, V7X_LANES), 1)
    logits = jnp.where(lane < N_EXPERTS, logits, -jnp.inf)

    r = lax.broadcasted_iota(jnp.int32, (tm, tm), 0)
    c = lax.broadcasted_iota(jnp.int32, (tm, tm), 1)
    tri = (c < r).astype(BF16)

    sel_mask = jnp.zeros((tm, V7X_LANES), F32)
    onehots, vals = [], []
    work = logits
    for _ in range(TOP_K):
        mx = jnp.max(work, axis=-1, keepdims=True)
        first = jnp.min(jnp.where(work == mx, lane, V7X_LANES), axis=-1, keepdims=True)
        oh = lane == first
        onehots.append(oh)
        vals.append(mx)
        sel_mask = sel_mask + oh.astype(F32)
        work = jnp.where(oh, -jnp.inf, work)

    prefix = jnp.dot(tri, sel_mask.astype(BF16), preferred_element_type=F32) + carry[...]
    carry[...] = carry[...] + jnp.sum(sel_mask, axis=0, keepdims=True)
    cnt_ref[...] = jnp.broadcast_to(carry[...], cnt_ref.shape)

    es = [jnp.exp(v - vals[0]) for v in vals]
    denom = es[0] + es[1] + es[2] + es[3]
    idx_out = jnp.zeros((tm, V7X_LANES), jnp.int32)
    wt_out = jnp.zeros((tm, V7X_LANES), F32)
    rank_out = jnp.zeros((tm, V7X_LANES), jnp.int32)
    for k in range(TOP_K):
        oh = onehots[k]
        e_k = jnp.sum(jnp.where(oh, lane, 0), axis=-1, keepdims=True)
        r_k = jnp.sum(jnp.where(oh, prefix, 0.0), axis=-1, keepdims=True).astype(jnp.int32)
        idx_out = jnp.where(lane == k, e_k, idx_out)
        wt_out = jnp.where(lane == k, es[k] / denom, wt_out)
        rank_out = jnp.where(lane == k, r_k, rank_out)
    idx_ref[...] = idx_out
    wt_ref[...] = wt_out
    rank_ref[...] = rank_out


def _router(h, g, w_router, b_router, *, tm):
    t, d = h.shape
    wr = jnp.zeros((d, V7X_LANES), F32).at[:, :N_EXPERTS].set(w_router)
    br = jnp.zeros((1, V7X_LANES), F32).at[0, :N_EXPERTS].set(b_router)
    row = lambda i: (i, 0)
    fixed = lambda i: (0, 0)
    kern = functools.partial(_router_kernel, tm=tm)
    return pl.pallas_call(
        kern,
        grid=(t // tm,),
        in_specs=[pl.BlockSpec((tm, d), row),
                  pl.BlockSpec((1, d), fixed),
                  pl.BlockSpec((d, V7X_LANES), fixed),
                  pl.BlockSpec((1, V7X_LANES), fixed)],
        out_specs=[pl.BlockSpec((tm, d // V7X_LANES, V7X_LANES), lambda i: (i, 0, 0)),
                   pl.BlockSpec((tm, V7X_LANES), row),
                   pl.BlockSpec((tm, V7X_LANES), row),
                   pl.BlockSpec((tm, V7X_LANES), row),
                   pl.BlockSpec((8, V7X_LANES), fixed)],
        out_shape=[jax.ShapeDtypeStruct((t, d // V7X_LANES, V7X_LANES), BF16),
                   jax.ShapeDtypeStruct((t, V7X_LANES), jnp.int32),
                   jax.ShapeDtypeStruct((t, V7X_LANES), F32),
                   jax.ShapeDtypeStruct((t, V7X_LANES), jnp.int32),
                   jax.ShapeDtypeStruct((8, V7X_LANES), F32)],
        scratch_shapes=[pltpu.VMEM((d, 2 * V7X_LANES), BF16),
                        pltpu.VMEM((1, V7X_LANES), F32)],
        compiler_params=_cparams(1),
        name="router",
    )(h, g.reshape(1, d), wr, br)


def _row_copy(src_hbm, dst, sem, src_row, dst_row):
    return pltpu.make_async_copy(src_hbm.at[pl.ds(src_row, 1)], dst.at[pl.ds(dst_row, 1)], sem)


def _gather_kernel(dest_ref, nu_ref, x_hbm, o_ref, tok_ref, buf, sem, *, bm):
    i = pl.program_id(0)
    n_used = nu_ref[0]

    def issue(block):
        slot = block % 2

        def body(rp, c):
            for p in range(2):
                r = rp * 2 + p
                _row_copy(x_hbm, buf.at[slot], sem.at[slot], tok_ref[block * bm + r], r).start(priority=p)
            return c

        lax.fori_loop(0, bm // 2, body, 0, unroll=4)

    @pl.when(i == 0)
    def _():
        def clear(r, c):
            tok_ref[r] = 0
            return c

        lax.fori_loop(0, tok_ref.shape[0], clear, 0, unroll=8)

        def invert(n, c):
            tok_ref[dest_ref[n]] = lax.shift_right_logical(n, TOP_K_SHIFT)
            return c

        lax.fori_loop(0, dest_ref.shape[0], invert, 0, unroll=8)
        issue(i)

    @pl.when(i + 1 < n_used)
    def _():
        issue(i + 1)

    @pl.when(i < n_used)
    def _():
        slot = i % 2
        pltpu.make_async_copy(x_hbm.at[pl.ds(0, bm)], buf.at[slot], sem.at[slot]).wait()
        o_ref[...] = buf[slot].reshape(o_ref.shape)

    @pl.when(i >= n_used)
    def _():
        o_ref[...] = jnp.zeros_like(o_ref)


def _gather_rows(dest, n_used, xn, *, bm, n_rows):
    _, sub, lanes = xn.shape
    d = sub * lanes
    kern = functools.partial(_gather_kernel, bm=bm)
    return pl.pallas_call(
        kern,
        grid_spec=pltpu.PrefetchScalarGridSpec(
            num_scalar_prefetch=2,
            grid=(n_rows // bm,),
            in_specs=[pl.BlockSpec(memory_space=pl.ANY)],
            out_specs=pl.BlockSpec((bm, d), lambda i, dst, nu: (i, 0)),
            scratch_shapes=[pltpu.SMEM((n_rows,), jnp.int32),
                            pltpu.VMEM((2, bm, sub, lanes), xn.dtype), pltpu.SemaphoreType.DMA((2,))]),
        out_shape=jax.ShapeDtypeStruct((n_rows, d), xn.dtype),
        compiler_params=_cparams(1),
        name="gather_rows",
    )(dest, n_used, xn)


def _weight_ring_step(sched, nj, copies):
    be, _, first, group, next_expert, n_groups, _ = sched
    j, i = pl.program_id(0), pl.program_id(1)
    seq = j * n_groups[0] + group[i]
    slot = seq % 2

    @pl.when(first[i] == 1)
    def _():
        @pl.when(seq == 0)
        def _():
            for c in copies(be[i], j, slot):
                c.start(priority=1)

        last_group = group[i] == n_groups[0] - 1

        @pl.when(jnp.logical_not(jnp.logical_and(last_group, j == nj - 1)))
        def _():
            for c in copies(next_expert[i], j + last_group.astype(jnp.int32), 1 - slot):
                c.start(priority=1)

        for c in copies(be[i], j, slot):
            c.wait()

    return slot


N_SCHED = 7


def _for_valid_rows(valid, bm, body):
    half = bm // 2

    @pl.when(valid > half)
    def _():
        body(bm)

    @pl.when(valid <= half)
    def _():
        body(half)


def _expert_up_kernel(*refs, tf, d_ff, nj):
    sched, (xs_ref, w_hbm, bg_ref, bl_ref, h_ref, wbuf, sem) = refs[:N_SCHED], refs[N_SCHED:]
    n_used = sched[1]
    i = pl.program_id(1)

    def copies(e, j, slot):
        col = pl.multiple_of(j * tf, tf)
        return (pltpu.make_async_copy(w_hbm.at[e, :, pl.ds(col, tf)], wbuf.at[slot, 0], sem.at[slot]),
                pltpu.make_async_copy(w_hbm.at[e, :, pl.ds(d_ff + col, tf)], wbuf.at[slot, 1], sem.at[slot]))

    @pl.when(i < n_used[0])
    def _():
        slot = _weight_ring_step(sched, nj, copies)
        bm = xs_ref.shape[0]

        def body(rows):
            x = xs_ref[:rows]
            g = jnp.dot(x, wbuf[slot, 0].astype(BF16), preferred_element_type=F32) + bg_ref[0]
            l = jnp.dot(x, wbuf[slot, 1].astype(BF16), preferred_element_type=F32) + bl_ref[0]
            glu = jnp.minimum(g, SWIGLU_LIMIT)
            lin = jnp.clip(l, -SWIGLU_LIMIT, SWIGLU_LIMIT)
            h_ref[:rows] = (glu * jax.nn.sigmoid(SWIGLU_ALPHA * glu) * (lin + 1.0)).astype(h_ref.dtype)
            if rows < bm:
                h_ref[rows:] = jnp.zeros((bm - rows, h_ref.shape[1]), h_ref.dtype)

        _for_valid_rows(sched[6][i], bm, body)

    @pl.when(i >= n_used[0])
    def _():
        h_ref[...] = jnp.zeros_like(h_ref)


def _expert_up(sched, xs, w_gate_up, b_gate_up, *, bm, tf):
    n_rows, d = xs.shape
    n_e, _, two_ff = w_gate_up.shape
    d_ff = two_ff // 2
    nj = d_ff // tf

    def blk(i, s):
        return jnp.minimum(i, s[1][0] - 1)

    kern = functools.partial(_expert_up_kernel, tf=tf, d_ff=d_ff, nj=nj)
    return pl.pallas_call(
        kern,
        grid_spec=pltpu.PrefetchScalarGridSpec(
            num_scalar_prefetch=len(sched),
            grid=(nj, n_rows // bm),
            in_specs=[pl.BlockSpec((bm, d), lambda j, i, *s: (blk(i, s), 0)),
                      pl.BlockSpec(memory_space=pl.ANY),
                      pl.BlockSpec((1, 1, tf), lambda j, i, *s: (s[0][blk(i, s)], 0, j)),
                      pl.BlockSpec((1, 1, tf), lambda j, i, *s: (s[0][blk(i, s)], 0, nj + j))],
            out_specs=pl.BlockSpec((bm, tf), lambda j, i, *s: (i, j)),
            scratch_shapes=[pltpu.VMEM((2, 2, d, tf), F32), pltpu.SemaphoreType.DMA((2,))]),
        out_shape=jax.ShapeDtypeStruct((n_rows, d_ff), BF16),
        compiler_params=_cparams(2),
        name="expert_up",
    )(*sched, xs, w_gate_up, b_gate_up.reshape(n_e, 1, two_ff), b_gate_up.reshape(n_e, 1, two_ff))


def _expert_down_kernel(*refs, tn, nj):
    sched, (h_ref, w_hbm, b_ref, y_ref, wbuf, sem) = refs[:N_SCHED], refs[N_SCHED:]
    n_used = sched[1]
    i = pl.program_id(1)

    def copies(e, j, slot):
        col = pl.multiple_of(j * tn, tn)
        return (pltpu.make_async_copy(w_hbm.at[e, :, pl.ds(col, tn)], wbuf.at[slot], sem.at[slot]),)

    @pl.when(i < n_used[0])
    def _():
        slot = _weight_ring_step(sched, nj, copies)
        bm = h_ref.shape[0]

        def body(rows):
            y_ref[:rows] = (jnp.dot(h_ref[:rows], wbuf[slot].astype(BF16), preferred_element_type=F32)
                            + b_ref[0])
            if rows < bm:
                y_ref[rows:] = jnp.zeros((bm - rows, y_ref.shape[1]), y_ref.dtype)

        _for_valid_rows(sched[6][i], bm, body)

    @pl.when(i >= n_used[0])
    def _():
        y_ref[...] = jnp.zeros_like(y_ref)


def _expert_down(sched, hidden, w_down, b_down, *, bm, tn):
    n_rows, d_ff = hidden.shape
    n_e, _, d = w_down.shape
    nj = d // tn

    def blk(i, s):
        return jnp.minimum(i, s[1][0] - 1)

    kern = functools.partial(_expert_down_kernel, tn=tn, nj=nj)
    return pl.pallas_call(
        kern,
        grid_spec=pltpu.PrefetchScalarGridSpec(
            num_scalar_prefetch=len(sched),
            grid=(nj, n_rows // bm),
            in_specs=[pl.BlockSpec((bm, d_ff), lambda j, i, *s: (blk(i, s), 0)),
                      pl.BlockSpec(memory_space=pl.ANY),
                      pl.BlockSpec((1, 1, tn), lambda j, i, *s: (s[0][blk(i, s)], 0, j))],
            out_specs=pl.BlockSpec((bm, tn), lambda j, i, *s: (i, j)),
            scratch_shapes=[pltpu.VMEM((2, d_ff, tn), F32), pltpu.SemaphoreType.DMA((2,))]),
        out_shape=jax.ShapeDtypeStruct((n_rows, d), F32),
        compiler_params=_cparams(2),
        name="expert_down",
    )(*sched, hidden, w_down, b_down.reshape(n_e, 1, d))


def _combine_kernel(dest_ref, h_ref, wt_ref, ys_hbm, g_ref, o_ref, buf, sem, *, tm):
    i = pl.program_id(0)

    def issue(tile):
        slot = tile % 2

        def body(r, c):
            for k in range(TOP_K):
                _row_copy(ys_hbm, buf.at[slot, k], sem.at[slot],
                          dest_ref[(tile * tm + r) * TOP_K + k], r).start(priority=k % 2)
            return c

        lax.fori_loop(0, tm, body, 0, unroll=4)

    @pl.when(i == 0)
    def _():
        issue(i)

    @pl.when(i + 1 < pl.num_programs(0))
    def _():
        issue(i + 1)

    slot = i % 2
    for k in range(TOP_K):
        pltpu.make_async_copy(ys_hbm.at[pl.ds(0, tm)], buf.at[slot, k], sem.at[slot]).wait()
    wt = wt_ref[...]
    y = h_ref[...]
    for k in range(TOP_K):
        y = y + wt[:, k:k + 1] * buf[slot, k]
    ms = jnp.mean(y * y, axis=-1, keepdims=True)
    o_ref[...] = y * lax.rsqrt(ms + EPS) * g_ref[...]


def _combine(dest_flat, h, wts, ys, g, *, tm):
    t, d = h.shape
    kern = functools.partial(_combine_kernel, tm=tm)
    return pl.pallas_call(
        kern,
        grid_spec=pltpu.PrefetchScalarGridSpec(
            num_scalar_prefetch=1,
            grid=(t // tm,),
            in_specs=[pl.BlockSpec((tm, d), lambda i, dst: (i, 0)),
                      pl.BlockSpec((tm, V7X_LANES), lambda i, dst: (i, 0)),
                      pl.BlockSpec(memory_space=pl.ANY),
                      pl.BlockSpec((1, d), lambda i, dst: (0, 0))],
            out_specs=pl.BlockSpec((tm, d), lambda i, dst: (i, 0)),
            scratch_shapes=[pltpu.VMEM((2, TOP_K, tm) + ys.shape[1:], ys.dtype),
                            pltpu.SemaphoreType.DMA((2,))]),
        out_shape=jax.ShapeDtypeStruct((t, d), F32),
        compiler_params=_cparams(1),
        name="combine",
    )(dest_flat, h, wts, ys, g.reshape(1, d))


class _Tiles:
    norm_rows = 256
    mm_cols = 1024
    qkv_rows = 1024
    mm_rows = 512
    chunk_q = 256
    chunk_heads = 8
    stick_q = 256
    stick_heads = 8
    router_rows = 256
    moe_block = 256
    up_cols = 512
    down_cols = 2048
    combine_rows = 128


def _moe_layout(idx, rank, counts, *, bm):
    t = idx.shape[0]
    n_assign = t * TOP_K
    n_blocks = n_assign // bm + N_EXPERTS
    padded = (counts + bm - 1) // bm * bm
    padded_end = jnp.cumsum(padded)
    padded_start = padded_end - padded
    dest = (padded_start[idx] + rank).reshape(-1).astype(jnp.int32)
    blk_start = jnp.arange(n_blocks, dtype=jnp.int32) * bm
    block_expert = jnp.minimum(
        jnp.sum((padded_end[None, :] <= blk_start[:, None]).astype(jnp.int32), axis=1), N_EXPERTS - 1)
    n_used = (padded_end[-1:] // bm).astype(jnp.int32)
    has = padded > 0
    e_ids = jnp.arange(N_EXPERTS, dtype=jnp.int32)
    later = has[None, :] & (e_ids[None, :] > e_ids[:, None])
    first_e = jnp.min(jnp.where(has, e_ids, N_EXPERTS))
    next_later = jnp.min(jnp.where(later, e_ids[None, :], N_EXPERTS), axis=1)
    next_of_expert = jnp.where(next_later < N_EXPERTS, next_later, first_e)
    group_of_expert = jnp.cumsum(has.astype(jnp.int32)) - 1
    used = blk_start < padded_end[-1]
    first = (used & (blk_start == padded_start[block_expert])).astype(jnp.int32)
    real_end = padded_start + counts
    valid_rows = jnp.clip(real_end[block_expert] - blk_start, 0, bm) * used
    sched = (block_expert.astype(jnp.int32), n_used, first,
             group_of_expert[block_expert].astype(jnp.int32),
             next_of_expert[block_expert].astype(jnp.int32),
             jnp.sum(has.astype(jnp.int32)).reshape(1),
             valid_rows.astype(jnp.int32))
    assert len(sched) == N_SCHED
    return dest, sched


def kernel(x, norm_mix_g, w_in, b_gate, rel_bias, w_branch_a, w_branch_b, w_out, norm_ffn_g,
           w_router, b_router, w_gate_up, b_gate_up, w_down, b_down, norm_final_g):
    batch, seq, d = x.shape
    t = batch * seq
    tl = _Tiles
    width_a = N_HEADS_A * HEAD_DIM
    width_b = N_HEADS_B * HEAD_DIM
    qkv_width = 3 * width_a + 3 * width_b
    xt = x.reshape(t, d)

    xn = _rmsnorm(xt, norm_mix_g, tm=tl.norm_rows, out_dtype=BF16)
    qkv = _mm(xn, w_in, col_off=0, n_out=qkv_width, tm=tl.qkv_rows, tn=tl.mm_cols,
              out_dtype=BF16, name="proj_qkv")
    gates = _mm(xn, w_in, col_off=qkv_width, n_out=2 * d, tm=tl.mm_rows, tn=tl.mm_cols,
                out_dtype=BF16, bias=b_gate, act="sigmoid", name="proj_gates")
    att_a = _chunk_attention(qkv, _chunk_bias_table(rel_bias, tl.chunk_q),
                             batch=batch, seq=seq, tq=tl.chunk_q, heads=tl.chunk_heads)
    att_b = _stick_attention(qkv, batch=batch, seq=seq, tq=tl.stick_q, heads=tl.stick_heads,
                             col_off=3 * width_a)
    merged = _branch_merge(att_a, att_b, w_branch_a, w_branch_b, gates, tm=tl.mm_rows, tn=tl.mm_cols)
    h = _mm(merged, w_out, col_off=0, n_out=d, tm=tl.mm_rows, tn=tl.mm_cols,
            out_dtype=F32, res=xt, name="out_proj")

    xn2, idx_l, wt_l, rank_l, cnt = _router(h, norm_ffn_g, w_router, b_router, tm=tl.router_rows)
    counts = cnt[0, :N_EXPERTS].astype(jnp.int32)
    dest, sched = _moe_layout(idx_l[:, :TOP_K], rank_l[:, :TOP_K], counts, bm=tl.moe_block)
    xs = _gather_rows(dest, sched[1], xn2, bm=tl.moe_block, n_rows=sched[0].shape[0] * tl.moe_block)
    hidden = _expert_up(sched, xs, w_gate_up, b_gate_up, bm=tl.moe_block, tf=tl.up_cols)
    ys = _expert_down(sched, hidden, w_down, b_down, bm=tl.moe_block, tn=tl.down_cols)
    y = _combine(dest, h, wt_l, ys, norm_final_g, tm=tl.combine_rows)
    return y.reshape(batch, seq, d)
```

```python
import functools
import math

import jax
import jax.numpy as jnp
import numpy as np
from jax import lax
from jax.experimental import pallas as pl
from jax.experimental.pallas import tpu as pltpu

F32 = jnp.float32
BF16 = jnp.bfloat16

HEAD_DIM = 128
N_HEADS_A = 16
N_HEADS_B = 16
CHUNK = 64
LEFT_CHUNKS = 8
REL_CLIP = 256
N_EXPERTS = 32
TOP_K = 4
TOP_K_SHIFT = 2
assert 1 << TOP_K_SHIFT == TOP_K
SWIGLU_ALPHA = 1.702
SWIGLU_LIMIT = 7.0
EPS = 1e-5
NEG = -1e30

V7X_LANES = 128
V7X_VMEM_BYTES = 64 * 1024 * 1024
VMEM_LIMIT = V7X_VMEM_BYTES - 4 * 1024 * 1024


def _cparams(n_axes):
    return pltpu.CompilerParams(
        dimension_semantics=("arbitrary",) * n_axes, vmem_limit_bytes=VMEM_LIMIT)


def _rmsnorm_kernel(x_ref, g_ref, o_ref):
    x = x_ref[...]
    ms = jnp.mean(x * x, axis=-1, keepdims=True)
    o_ref[...] = (x * lax.rsqrt(ms + EPS) * g_ref[...]).astype(o_ref.dtype)


def _rmsnorm(x, g, *, tm, out_dtype):
    t, d = x.shape
    return pl.pallas_call(
        _rmsnorm_kernel,
        grid=(t // tm,),
        in_specs=[pl.BlockSpec((tm, d), lambda i: (i, 0)),
                  pl.BlockSpec((1, d), lambda i: (0, 0))],
        out_specs=pl.BlockSpec((tm, d), lambda i: (i, 0)),
        out_shape=jax.ShapeDtypeStruct((t, d), out_dtype),
        compiler_params=_cparams(1),
        name="rmsnorm",
    )(x, g.reshape(1, d))


def _mm_kernel(*refs, has_bias, has_res, act):
    a_ref, w_ref = refs[0], refs[1]
    pos = 2
    b_ref = r_ref = None
    if has_bias:
        b_ref = refs[pos]
        pos += 1
    if has_res:
        r_ref = refs[pos]
        pos += 1
    o_ref = refs[pos]

    acc = jnp.dot(a_ref[...], w_ref[...].astype(BF16), preferred_element_type=F32)
    if has_bias:
        acc = acc + b_ref[...]
    if act == "sigmoid":
        acc = jax.nn.sigmoid(acc)
    if has_res:
        acc = acc + r_ref[...]
    o_ref[...] = acc.astype(o_ref.dtype)


def _mm(a, w, *, col_off, n_out, tm, tn, out_dtype, bias=None, res=None, act=None, name):
    m, k = a.shape
    assert w.shape[0] == k and col_off % tn == 0 and n_out % tn == 0 and m % tm == 0
    off = col_off // tn
    in_specs = [pl.BlockSpec((tm, k), lambda j, i: (i, 0)),
                pl.BlockSpec((k, tn), lambda j, i: (0, j + off))]
    args = [a, w]
    if bias is not None:
        in_specs.append(pl.BlockSpec((1, tn), lambda j, i: (0, j)))
        args.append(bias.reshape(1, n_out))
    if res is not None:
        in_specs.append(pl.BlockSpec((tm, tn), lambda j, i: (i, j)))
        args.append(res)
    kern = functools.partial(_mm_kernel, has_bias=bias is not None, has_res=res is not None, act=act)
    return pl.pallas_call(
        kern,
        grid=(n_out // tn, m // tm),
        in_specs=in_specs,
        out_specs=pl.BlockSpec((tm, tn), lambda j, i: (i, j)),
        out_shape=jax.ShapeDtypeStruct((m, n_out), out_dtype),
        compiler_params=_cparams(2),
        name=name,
    )(*args)


def _branch_kernel(a_ref, b_ref, wa_ref, wb_ref, ga_ref, gb_ref, o_ref):
    ya = jnp.dot(a_ref[...], wa_ref[...].astype(BF16), preferred_element_type=F32)
    yb = jnp.dot(b_ref[...], wb_ref[...].astype(BF16), preferred_element_type=F32)
    o_ref[...] = (ga_ref[...].astype(F32) * ya + gb_ref[...].astype(F32) * yb).astype(o_ref.dtype)


def _branch_merge(att_a, att_b, wa, wb, gates, *, tm, tn):
    m, ka = att_a.shape
    kb = att_b.shape[1]
    d = wa.shape[1]
    nj = d // tn
    return pl.pallas_call(
        _branch_kernel,
        grid=(nj, m // tm),
        in_specs=[pl.BlockSpec((tm, ka), lambda j, i: (i, 0)),
                  pl.BlockSpec((tm, kb), lambda j, i: (i, 0)),
                  pl.BlockSpec((ka, tn), lambda j, i: (0, j)),
                  pl.BlockSpec((kb, tn), lambda j, i: (0, j)),
                  pl.BlockSpec((tm, tn), lambda j, i: (i, j)),
                  pl.BlockSpec((tm, tn), lambda j, i: (i, j + nj))],
        out_specs=pl.BlockSpec((tm, tn), lambda j, i: (i, j)),
        out_shape=jax.ShapeDtypeStruct((m, d), BF16),
        compiler_params=_cparams(2),
        name="branch_merge",
    )(att_a, att_b, wa, wb, gates, gates)


def _chunk_bias_table(rel_bias, tq):
    left = LEFT_CHUNKS * CHUNK
    nk = 3 * tq
    assert left == 2 * tq
    qi = np.arange(tq)[:, None]
    kw = np.arange(nk)[None, :]
    krel = kw - left
    qc = qi // CHUNK
    kc = np.floor_divide(krel, CHUNK)
    valid = (kc >= qc - LEFT_CHUNKS) & (kc <= qc)
    rel_min, rel_max = left - nk + 1, left + tq - 1
    assert -REL_CLIP <= rel_min and rel_max > REL_CLIP
    rb = rel_bias.astype(F32).T
    nh = rb.shape[0]
    by_rel = jnp.concatenate(
        [rb[:, rel_min + REL_CLIP:], jnp.broadcast_to(rb[:, -1:], (nh, rel_max - REL_CLIP))], axis=1)
    period = rel_max - rel_min + 1
    w = jnp.concatenate([by_rel[:, :nk][:, ::-1], by_rel[:, nk:][:, ::-1]], axis=1)
    bias = jnp.tile(w, (1, tq))[:, :tq * (period - 1)].reshape(nh, tq, period - 1)[:, :, :nk]
    return jnp.where(jnp.asarray(valid)[None], bias, NEG)


def _chunk_attn_kernel(q_ref, k_ref, v_ref, bias_ref, o_ref, *, tq, heads, scale):
    i = pl.program_id(2)
    starts = [pl.multiple_of(jnp.maximum(i - 2 + j, 0) * tq, tq) for j in range(3)]
    for h in range(heads):
        hs = slice(h * HEAD_DIM, (h + 1) * HEAD_DIM)
        q = q_ref[:, hs]
        s_blocks = []
        for j in range(3):
            k = k_ref[pl.ds(starts[j], tq), hs]
            s = lax.dot_general(q, k, (((1,), (1,)), ((), ())), preferred_element_type=F32) * scale
            s = s + bias_ref[h, :, j * tq:(j + 1) * tq]
            if j < 2:
                s = jnp.where(i - 2 + j >= 0, s, NEG)
            s_blocks.append(s)
        m = jnp.maximum(jnp.maximum(jnp.max(s_blocks[0], axis=-1, keepdims=True),
                                    jnp.max(s_blocks[1], axis=-1, keepdims=True)),
                        jnp.max(s_blocks[2], axis=-1, keepdims=True))
        l = jnp.zeros_like(m)
        acc = jnp.zeros((tq, HEAD_DIM), F32)
        for j in range(3):
            p = jnp.exp(s_blocks[j] - m)
            l = l + jnp.sum(p, axis=-1, keepdims=True)
            v = v_ref[pl.ds(starts[j], tq), hs]
            acc = acc + jnp.dot(p.astype(BF16), v, preferred_element_type=F32)
        o_ref[:, hs] = (acc / l).astype(o_ref.dtype)


def _chunk_attention(qkv, bias_tab, *, batch, seq, tq, heads):
    nq = seq // tq
    nh = N_HEADS_A
    assert nh % heads == 0
    ng = nh // heads
    width = heads * HEAD_DIM
    kern = functools.partial(_chunk_attn_kernel, tq=tq, heads=heads, scale=1.0 / math.sqrt(HEAD_DIM))
    return pl.pallas_call(
        kern,
        grid=(batch, ng, nq),
        in_specs=[pl.BlockSpec((tq, width), lambda b, g, i: (b * nq + i, g)),
                  pl.BlockSpec((seq, width), lambda b, g, i: (b, ng + g)),
                  pl.BlockSpec((seq, width), lambda b, g, i: (b, 2 * ng + g)),
                  pl.BlockSpec((heads, tq, 3 * tq), lambda b, g, i: (g, 0, 0))],
        out_specs=pl.BlockSpec((tq, width), lambda b, g, i: (b * nq + i, g)),
        out_shape=jax.ShapeDtypeStruct((batch * seq, nh * HEAD_DIM), BF16),
        compiler_params=_cparams(3),
        name="chunk_attention",
    )(qkv, qkv, qkv, bias_tab)


def _suffix_matrix(tk):
    j = np.arange(tk)[:, None]
    s = np.arange(tk)[None, :]
    later = -(j > s).astype(np.float32)
    half = np.concatenate([later, -np.ones((tk, tk), np.float32)], axis=1)
    return jnp.asarray(np.concatenate([half, half], axis=0), BF16)


def _softplus(z):
    return jnp.maximum(z, 0.0) + jnp.log(1.0 + jnp.exp(-jnp.abs(z)))


def _stick_kernel(q_ref, k_ref, v_ref, u_ref, o_ref, acc_ref, run_ref, *, tq, tk, heads, scale):
    i = pl.program_id(2)
    u = u_ref[...]
    row = lax.broadcasted_iota(jnp.int32, (tq, tq), 0)
    col = lax.broadcasted_iota(jnp.int32, (tq, tq), 1)
    causal = col < row

    def span(h, start, diagonal):
        hs = slice(h * HEAD_DIM, (h + 1) * HEAD_DIM)
        q = q_ref[:, hs]
        k = k_ref[pl.ds(start, tq), hs]
        v = v_ref[pl.ds(start, tq), hs]
        z = lax.dot_general(q, k, (((1,), (1,)), ((), ())), preferred_element_type=F32) * scale
        sp = _softplus(z)
        neg_log_keep = jnp.where(causal, sp, 0.0) if diagonal else sp
        run = run_ref[h]
        later = [None, None]
        for half in (1, 0):
            s = neg_log_keep[:, half * tk:(half + 1) * tk]
            hi = s.astype(BF16)
            lo = (s - hi.astype(F32)).astype(BF16)
            sums = jnp.dot(jnp.concatenate([hi, lo], axis=1), u, preferred_element_type=F32)
            later[half] = sums[:, :tk] + run
            run = run + sums[:, tk:]
        run_ref[h] = run
        a = jnp.exp(z - sp + jnp.concatenate(later, axis=1))
        if diagonal:
            a = jnp.where(causal, a, 0.0)
        acc_ref[h] += jnp.dot(a.astype(BF16), v, preferred_element_type=F32)

    acc_ref[...] = jnp.zeros_like(acc_ref)
    run_ref[...] = jnp.zeros_like(run_ref)
    for h in range(heads):
        span(h, pl.multiple_of(i * tq, tq), True)

    def body(t, c):
        start = pl.multiple_of((i - 1 - t) * tq, tq)
        for h in range(heads):
            span(h, start, False)
        return c

    lax.fori_loop(0, i, body, 0)
    for h in range(heads):
        o_ref[:, h * HEAD_DIM:(h + 1) * HEAD_DIM] = acc_ref[h].astype(o_ref.dtype)


def _stick_attention(qkv, *, batch, seq, tq, heads, col_off):
    nq = seq // tq
    nh = N_HEADS_B
    tk = tq // 2
    width = heads * HEAD_DIM
    assert col_off % width == 0 and nh % heads == 0
    c0 = col_off // width
    ng = nh // heads
    kern = functools.partial(_stick_kernel, tq=tq, tk=tk, heads=heads, scale=1.0 / math.sqrt(HEAD_DIM))
    return pl.pallas_call(
        kern,
        grid=(batch, ng, nq),
        in_specs=[pl.BlockSpec((tq, width), lambda b, g, i: (b * nq + i, c0 + g)),
                  pl.BlockSpec((seq, width), lambda b, g, i: (b, c0 + ng + g)),
                  pl.BlockSpec((seq, width), lambda b, g, i: (b, c0 + 2 * ng + g)),
                  pl.BlockSpec((2 * tk, 2 * tk), lambda b, g, i: (0, 0))],
        out_specs=pl.BlockSpec((tq, width), lambda b, g, i: (b * nq + i, g)),
        out_shape=jax.ShapeDtypeStruct((batch * seq, nh * HEAD_DIM), BF16),
        scratch_shapes=[pltpu.VMEM((heads, tq, HEAD_DIM), F32), pltpu.VMEM((heads, tq, tk), F32)],
        compiler_params=_cparams(3),
        name="stick_attention",
    )(qkv, qkv, qkv, _suffix_matrix(tk))


def _router_kernel(h_ref, g_ref, wr_ref, br_ref, xn_ref, idx_ref, wt_ref, rank_ref, cnt_ref,
                   wr_split, carry, *, tm):
    step = pl.program_id(0)

    @pl.when(step == 0)
    def _():
        w = wr_ref[...]
        hi = w.astype(BF16)
        wr_split[:, :V7X_LANES] = hi
        wr_split[:, V7X_LANES:] = (w - hi.astype(F32)).astype(BF16)
        carry[...] = jnp.zeros_like(carry)

    x = h_ref[...]
    ms = jnp.mean(x * x, axis=-1, keepdims=True)
    xn = x * lax.rsqrt(ms + EPS) * g_ref[...]
    xn_ref[...] = xn.reshape(xn_ref.shape).astype(xn_ref.dtype)
    xh = xn.astype(BF16)
    xl = (xn - xh.astype(F32)).astype(BF16)
    both = jnp.dot(xh, wr_split[...], preferred_element_type=F32)
    logits = (both[:, :V7X_LANES] + both[:, V7X_LANES:]
              + jnp.dot(xl, wr_split[:, :V7X_LANES], preferred_element_type=F32)) + br_ref[...]
    lane = lax.broadcasted_iota(jnp.int32, (tm, V7X_LANES), 1)
    logits = jnp.where(lane < N_EXPERTS, logits, -jnp.inf)

    r = lax.broadcasted_iota(jnp.int32, (tm, tm), 0)
    c = lax.broadcasted_iota(jnp.int32, (tm, tm), 1)
    tri = (c < r).astype(BF16)

    sel_mask = jnp.zeros((tm, V7X_LANES), F32)
    onehots, vals = [], []
    work = logits
    for _ in range(TOP_K):
        mx = jnp.max(work, axis=-1, keepdims=True)
        first = jnp.min(jnp.where(work == mx, lane, V7X_LANES), axis=-1, keepdims=True)
        oh = lane == first
        onehots.append(oh)
        vals.append(mx)
        sel_mask = sel_mask + oh.astype(F32)
        work = jnp.where(oh, -jnp.inf, work)

    prefix = jnp.dot(tri, sel_mask.astype(BF16), preferred_element_type=F32) + carry[...]
    carry[...] = carry[...] + jnp.sum(sel_mask, axis=0, keepdims=True)
    cnt_ref[...] = jnp.broadcast_to(carry[...], cnt_ref.shape)

    es = [jnp.exp(v - vals[0]) for v in vals]
    denom = es[0] + es[1] + es[2] + es[3]
    idx_out = jnp.zeros((tm, V7X_LANES), jnp.int32)
    wt_out = jnp.zeros((tm, V7X_LANES), F32)
    rank_out = jnp.zeros((tm, V7X_LANES), jnp.int32)
    for k in range(TOP_K):
        oh = onehots[k]
        e_k = jnp.sum(jnp.where(oh, lane, 0), axis=-1, keepdims=True)
        r_k = jnp.sum(jnp.where(oh, prefix, 0.0), axis=-1, keepdims=True).astype(jnp.int32)
        idx_out = jnp.where(lane == k, e_k, idx_out)
        wt_out = jnp.where(lane == k, es[k] / denom, wt_out)
        rank_out = jnp.where(lane == k, r_k, rank_out)
    idx_ref[...] = idx_out
    wt_ref[...] = wt_out
    rank_ref[...] = rank_out


def _router(h, g, w_router, b_router, *, tm):
    t, d = h.shape
    wr = jnp.zeros((d, V7X_LANES), F32).at[:, :N_EXPERTS].set(w_router)
    br = jnp.zeros((1, V7X_LANES), F32).at[0, :N_EXPERTS].set(b_router)
    row = lambda i: (i, 0)
    fixed = lambda i: (0, 0)
    kern = functools.partial(_router_kernel, tm=tm)
    return pl.pallas_call(
        kern,
        grid=(t // tm,),
        in_specs=[pl.BlockSpec((tm, d), row),
                  pl.BlockSpec((1, d), fixed),
                  pl.BlockSpec((d, V7X_LANES), fixed),
                  pl.BlockSpec((1, V7X_LANES), fixed)],
        out_specs=[pl.BlockSpec((tm, d // V7X_LANES, V7X_LANES), lambda i: (i, 0, 0)),
                   pl.BlockSpec((tm, V7X_LANES), row),
                   pl.BlockSpec((tm, V7X_LANES), row),
                   pl.BlockSpec((tm, V7X_LANES), row),
                   pl.BlockSpec((8, V7X_LANES), fixed)],
        out_shape=[jax.ShapeDtypeStruct((t, d // V7X_LANES, V7X_LANES), BF16),
                   jax.ShapeDtypeStruct((t, V7X_LANES), jnp.int32),
                   jax.ShapeDtypeStruct((t, V7X_LANES), F32),
                   jax.ShapeDtypeStruct((t, V7X_LANES), jnp.int32),
                   jax.ShapeDtypeStruct((8, V7X_LANES), F32)],
        scratch_shapes=[pltpu.VMEM((d, 2 * V7X_LANES), BF16),
                        pltpu.VMEM((1, V7X_LANES), F32)],
        compiler_params=_cparams(1),
        name="router",
    )(h, g.reshape(1, d), wr, br)


def _row_copy(src_hbm, dst, sem, src_row, dst_row):
    return pltpu.make_async_copy(src_hbm.at[pl.ds(src_row, 1)], dst.at[pl.ds(dst_row, 1)], sem)


def _for_valid_rows(valid, bm, unit, body):
    for rows in range(unit, bm + 1, unit):
        cond = valid <= rows
        if rows > unit:
            cond = jnp.logical_and(cond, valid > rows - unit)
        pl.when(cond)(functools.partial(body, rows))


def _gather_kernel(dest_ref, nu_ref, valid_ref, x_hbm, o_ref, tok_ref, buf, sem, *, bm, unit):
    i = pl.program_id(0)
    n_used = nu_ref[0]

    def issue(block):
        slot = block % 2

        def issue_rows(rows):
            def body(rp, c):
                for p in range(2):
                    r = rp * 2 + p
                    _row_copy(x_hbm, buf.at[slot], sem.at[slot],
                              tok_ref[block * bm + r], r).start(priority=p)
                return c

            lax.fori_loop(0, rows // 2, body, 0, unroll=4)

        _for_valid_rows(valid_ref[block], bm, unit, issue_rows)

    @pl.when(i == 0)
    def _():
        buf[...] = jnp.zeros_like(buf)

        def clear(r, c):
            tok_ref[r] = 0
            return c

        lax.fori_loop(0, tok_ref.shape[0], clear, 0, unroll=8)

        def invert(n, c):
            tok_ref[dest_ref[n]] = lax.shift_right_logical(n, TOP_K_SHIFT)
            return c

        lax.fori_loop(0, dest_ref.shape[0], invert, 0, unroll=8)
        issue(i)

    @pl.when(i + 1 < n_used)
    def _():
        issue(i + 1)

    @pl.when(i < n_used)
    def _():
        slot = i % 2

        def wait_rows(rows):
            pltpu.make_async_copy(x_hbm.at[pl.ds(0, rows)], buf.at[slot, pl.ds(0, rows)],
                                  sem.at[slot]).wait()

        _for_valid_rows(valid_ref[i], bm, unit, wait_rows)
        o_ref[...] = buf[slot].reshape(o_ref.shape)

    @pl.when(i >= n_used)
    def _():
        o_ref[...] = jnp.zeros_like(o_ref)


def _gather_rows(dest, n_used, valid_rows, xn, *, bm, unit):
    n_rows = valid_rows.shape[0] * bm
    _, sub, lanes = xn.shape
    d = sub * lanes
    kern = functools.partial(_gather_kernel, bm=bm, unit=unit)
    return pl.pallas_call(
        kern,
        grid_spec=pltpu.PrefetchScalarGridSpec(
            num_scalar_prefetch=3,
            grid=(n_rows // bm,),
            in_specs=[pl.BlockSpec(memory_space=pl.ANY)],
            out_specs=pl.BlockSpec((bm, d), lambda i, dst, nu, vr: (i, 0)),
            scratch_shapes=[pltpu.SMEM((n_rows,), jnp.int32),
                            pltpu.VMEM((2, bm, sub, lanes), xn.dtype), pltpu.SemaphoreType.DMA((2,))]),
        out_shape=jax.ShapeDtypeStruct((n_rows, d), xn.dtype),
        compiler_params=_cparams(1),
        name="gather_rows",
    )(dest, n_used, valid_rows, xn)


def _weight_ring_step(sched, nj, copies):
    be, _, first, group, next_expert, n_groups, _ = sched
    j, i = pl.program_id(0), pl.program_id(1)
    seq = j * n_groups[0] + group[i]
    slot = seq % 2

    @pl.when(first[i] == 1)
    def _():
        @pl.when(seq == 0)
        def _():
            for c in copies(be[i], j, slot):
                c.start(priority=1)

        last_group = group[i] == n_groups[0] - 1

        @pl.when(jnp.logical_not(jnp.logical_and(last_group, j == nj - 1)))
        def _():
            for c in copies(next_expert[i], j + last_group.astype(jnp.int32), 1 - slot):
                c.start(priority=1)

        for c in copies(be[i], j, slot):
            c.wait()

    return slot


N_SCHED = 7


def _expert_up_kernel(*refs, tf, d_ff, nj, unit):
    sched, (xs_ref, w_hbm, bg_ref, bl_ref, h_ref, wbuf, sem) = refs[:N_SCHED], refs[N_SCHED:]
    n_used = sched[1]
    i = pl.program_id(1)

    def copies(e, j, slot):
        col = pl.multiple_of(j * tf, tf)
        return (pltpu.make_async_copy(w_hbm.at[e, :, pl.ds(col, tf)], wbuf.at[slot, 0], sem.at[slot]),
                pltpu.make_async_copy(w_hbm.at[e, :, pl.ds(d_ff + col, tf)], wbuf.at[slot, 1], sem.at[slot]))

    @pl.when(i < n_used[0])
    def _():
        slot = _weight_ring_step(sched, nj, copies)
        bm = xs_ref.shape[0]

        def body(rows):
            x = xs_ref[:rows]
            g = jnp.dot(x, wbuf[slot, 0].astype(BF16), preferred_element_type=F32) + bg_ref[0]
            l = jnp.dot(x, wbuf[slot, 1].astype(BF16), preferred_element_type=F32) + bl_ref[0]
            glu = jnp.minimum(g, SWIGLU_LIMIT)
            lin = jnp.clip(l, -SWIGLU_LIMIT, SWIGLU_LIMIT)
            h_ref[:rows] = (glu * jax.nn.sigmoid(SWIGLU_ALPHA * glu) * (lin + 1.0)).astype(h_ref.dtype)
            if rows < bm:
                h_ref[rows:] = jnp.zeros((bm - rows, h_ref.shape[1]), h_ref.dtype)

        _for_valid_rows(sched[6][i], bm, unit, body)

    @pl.when(i >= n_used[0])
    def _():
        h_ref[...] = jnp.zeros_like(h_ref)


def _expert_up(sched, xs, w_gate_up, b_gate_up, *, bm, unit, tf):
    n_rows, d = xs.shape
    n_e, _, two_ff = w_gate_up.shape
    d_ff = two_ff // 2
    nj = d_ff // tf

    def blk(i, s):
        return jnp.minimum(i, s[1][0] - 1)

    kern = functools.partial(_expert_up_kernel, tf=tf, d_ff=d_ff, nj=nj, unit=unit)
    return pl.pallas_call(
        kern,
        grid_spec=pltpu.PrefetchScalarGridSpec(
            num_scalar_prefetch=len(sched),
            grid=(nj, n_rows // bm),
            in_specs=[pl.BlockSpec((bm, d), lambda j, i, *s: (blk(i, s), 0)),
                      pl.BlockSpec(memory_space=pl.ANY),
                      pl.BlockSpec((1, 1, tf), lambda j, i, *s: (s[0][blk(i, s)], 0, j)),
                      pl.BlockSpec((1, 1, tf), lambda j, i, *s: (s[0][blk(i, s)], 0, nj + j))],
            out_specs=pl.BlockSpec((bm, tf), lambda j, i, *s: (i, j)),
            scratch_shapes=[pltpu.VMEM((2, 2, d, tf), F32), pltpu.SemaphoreType.DMA((2,))]),
        out_shape=jax.ShapeDtypeStruct((n_rows, d_ff), BF16),
        compiler_params=_cparams(2),
        name="expert_up",
    )(*sched, xs, w_gate_up, b_gate_up.reshape(n_e, 1, two_ff), b_gate_up.reshape(n_e, 1, two_ff))


def _expert_down_kernel(*refs, tn, nj, unit):
    sched, (h_ref, w_hbm, b_ref, y_ref, wbuf, sem) = refs[:N_SCHED], refs[N_SCHED:]
    n_used = sched[1]
    i = pl.program_id(1)

    def copies(e, j, slot):
        col = pl.multiple_of(j * tn, tn)
        return (pltpu.make_async_copy(w_hbm.at[e, :, pl.ds(col, tn)], wbuf.at[slot], sem.at[slot]),)

    @pl.when(i < n_used[0])
    def _():
        slot = _weight_ring_step(sched, nj, copies)
        bm = h_ref.shape[0]

        def body(rows):
            y_ref[:rows] = (jnp.dot(h_ref[:rows], wbuf[slot].astype(BF16), preferred_element_type=F32)
                            + b_ref[0])
            if rows < bm:
                y_ref[rows:] = jnp.zeros((bm - rows, y_ref.shape[1]), y_ref.dtype)

        _for_valid_rows(sched[6][i], bm, unit, body)

    @pl.when(i >= n_used[0])
    def _():
        y_ref[...] = jnp.zeros_like(y_ref)


def _expert_down(sched, hidden, w_down, b_down, *, bm, unit, tn):
    n_rows, d_ff = hidden.shape
    n_e, _, d = w_down.shape
    nj = d // tn

    def blk(i, s):
        return jnp.minimum(i, s[1][0] - 1)

    kern = functools.partial(_expert_down_kernel, tn=tn, nj=nj, unit=unit)
    return pl.pallas_call(
        kern,
        grid_spec=pltpu.PrefetchScalarGridSpec(
            num_scalar_prefetch=len(sched),
            grid=(nj, n_rows // bm),
            in_specs=[pl.BlockSpec((bm, d_ff), lambda j, i, *s: (blk(i, s), 0)),
                      pl.BlockSpec(memory_space=pl.ANY),
                      pl.BlockSpec((1, 1, tn), lambda j, i, *s: (s[0][blk(i, s)], 0, j))],
            out_specs=pl.BlockSpec((bm, tn), lambda j, i, *s: (i, j)),
            scratch_shapes=[pltpu.VMEM((2, d_ff, tn), F32), pltpu.SemaphoreType.DMA((2,))]),
        out_shape=jax.ShapeDtypeStruct((n_rows, d), F32),
        compiler_params=_cparams(2),
        name="expert_down",
    )(*sched, hidden, w_down, b_down.reshape(n_e, 1, d))


def _combine_kernel(dest_ref, h_ref, wt_ref, ys_hbm, g_ref, o_ref, buf, sem, *, tm):
    i = pl.program_id(0)

    def issue(tile):
        slot = tile % 2

        def body(r, c):
            for k in range(TOP_K):
                _row_copy(ys_hbm, buf.at[slot, k], sem.at[slot],
                          dest_ref[(tile * tm + r) * TOP_K + k], r).start(priority=k % 2)
            return c

        lax.fori_loop(0, tm, body, 0, unroll=4)

    @pl.when(i == 0)
    def _():
        issue(i)

    @pl.when(i + 1 < pl.num_programs(0))
    def _():
        issue(i + 1)

    slot = i % 2
    for k in range(TOP_K):
        pltpu.make_async_copy(ys_hbm.at[pl.ds(0, tm)], buf.at[slot, k], sem.at[slot]).wait()
    wt = wt_ref[...]
    y = h_ref[...]
    for k in range(TOP_K):
        y = y + wt[:, k:k + 1] * buf[slot, k]
    ms = jnp.mean(y * y, axis=-1, keepdims=True)
    o_ref[...] = y * lax.rsqrt(ms + EPS) * g_ref[...]


def _combine(dest_flat, h, wts, ys, g, *, tm):
    t, d = h.shape
    kern = functools.partial(_combine_kernel, tm=tm)
    return pl.pallas_call(
        kern,
        grid_spec=pltpu.PrefetchScalarGridSpec(
            num_scalar_prefetch=1,
            grid=(t // tm,),
            in_specs=[pl.BlockSpec((tm, d), lambda i, dst: (i, 0)),
                      pl.BlockSpec((tm, V7X_LANES), lambda i, dst: (i, 0)),
                      pl.BlockSpec(memory_space=pl.ANY),
                      pl.BlockSpec((1, d), lambda i, dst: (0, 0))],
            out_specs=pl.BlockSpec((tm, d), lambda i, dst: (i, 0)),
            scratch_shapes=[pltpu.VMEM((2, TOP_K, tm) + ys.shape[1:], ys.dtype),
                            pltpu.SemaphoreType.DMA((2,))]),
        out_shape=jax.ShapeDtypeStruct((t, d), F32),
        compiler_params=_cparams(1),
        name="combine",
    )(dest_flat, h, wts, ys, g.reshape(1, d))


class _Tiles:
    norm_rows = 256
    mm_cols = 1024
    qkv_rows = 1024
    mm_rows = 512
    chunk_q = 256
    chunk_heads = 8
    stick_q = 256
    stick_heads = 8
    router_rows = 256
    moe_block = 512
    moe_unit = 128
    up_cols = 512
    down_cols = 2048
    combine_rows = 128


def _moe_layout(idx, rank, counts, *, bm):
    t = idx.shape[0]
    n_assign = t * TOP_K
    n_blocks = n_assign // bm + N_EXPERTS
    padded = (counts + bm - 1) // bm * bm
    padded_end = jnp.cumsum(padded)
    padded_start = padded_end - padded
    dest = (padded_start[idx] + rank).reshape(-1).astype(jnp.int32)
    blk_start = jnp.arange(n_blocks, dtype=jnp.int32) * bm
    block_expert = jnp.minimum(
        jnp.sum((padded_end[None, :] <= blk_start[:, None]).astype(jnp.int32), axis=1), N_EXPERTS - 1)
    n_used = (padded_end[-1:] // bm).astype(jnp.int32)
    has = padded > 0
    e_ids = jnp.arange(N_EXPERTS, dtype=jnp.int32)
    later = has[None, :] & (e_ids[None, :] > e_ids[:, None])
    first_e = jnp.min(jnp.where(has, e_ids, N_EXPERTS))
    next_later = jnp.min(jnp.where(later, e_ids[None, :], N_EXPERTS), axis=1)
    next_of_expert = jnp.where(next_later < N_EXPERTS, next_later, first_e)
    group_of_expert = jnp.cumsum(has.astype(jnp.int32)) - 1
    used = blk_start < padded_end[-1]
    first = (used & (blk_start == padded_start[block_expert])).astype(jnp.int32)
    real_end = padded_start + counts
    valid_rows = jnp.clip(real_end[block_expert] - blk_start, 0, bm) * used
    sched = (block_expert.astype(jnp.int32), n_used, first,
             group_of_expert[block_expert].astype(jnp.int32),
             next_of_expert[block_expert].astype(jnp.int32),
             jnp.sum(has.astype(jnp.int32)).reshape(1),
             valid_rows.astype(jnp.int32))
    assert len(sched) == N_SCHED
    return dest, sched


def kernel(x, norm_mix_g, w_in, b_gate, rel_bias, w_branch_a, w_branch_b, w_out, norm_ffn_g,
           w_router, b_router, w_gate_up, b_gate_up, w_down, b_down, norm_final_g):
    batch, seq, d = x.shape
    t = batch * seq
    tl = _Tiles
    width_a = N_HEADS_A * HEAD_DIM
    width_b = N_HEADS_B * HEAD_DIM
    qkv_width = 3 * width_a + 3 * width_b
    xt = x.reshape(t, d)

    xn = _rmsnorm(xt, norm_mix_g, tm=tl.norm_rows, out_dtype=BF16)
    qkv = _mm(xn, w_in, col_off=0, n_out=qkv_width, tm=tl.qkv_rows, tn=tl.mm_cols,
              out_dtype=BF16, name="proj_qkv")
    gates = _mm(xn, w_in, col_off=qkv_width, n_out=2 * d, tm=tl.mm_rows, tn=tl.mm_cols,
                out_dtype=BF16, bias=b_gate, act="sigmoid", name="proj_gates")
    att_a = _chunk_attention(qkv, _chunk_bias_table(rel_bias, tl.chunk_q),
                             batch=batch, seq=seq, tq=tl.chunk_q, heads=tl.chunk_heads)
    att_b = _stick_attention(qkv, batch=batch, seq=seq, tq=tl.stick_q, heads=tl.stick_heads,
                             col_off=3 * width_a)
    merged = _branch_merge(att_a, att_b, w_branch_a, w_branch_b, gates, tm=tl.mm_rows, tn=tl.mm_cols)
    h = _mm(merged, w_out, col_off=0, n_out=d, tm=tl.mm_rows, tn=tl.mm_cols,
            out_dtype=F32, res=xt, name="out_proj")

    xn2, idx_l, wt_l, rank_l, cnt = _router(h, norm_ffn_g, w_router, b_router, tm=tl.router_rows)
    counts = cnt[0, :N_EXPERTS].astype(jnp.int32)
    dest, sched = _moe_layout(idx_l[:, :TOP_K], rank_l[:, :TOP_K], counts, bm=tl.moe_block)
    xs = _gather_rows(dest, sched[1], sched[6], xn2, bm=tl.moe_block, unit=tl.moe_unit)
    hidden = _expert_up(sched, xs, w_gate_up, b_gate_up, bm=tl.moe_block, unit=tl.moe_unit, tf=tl.up_cols)
    ys = _expert_down(sched, hidden, w_down, b_down, bm=tl.moe_block, unit=tl.moe_unit, tn=tl.down_cols)
    y = _combine(dest, h, wt_l, ys, norm_final_g, tm=tl.combine_rows)
    return y.reshape(batch, seq, d)
```

```python
import functools
import math

import jax
import jax.numpy as jnp
import numpy as np
from jax import lax
from jax.experimental import pallas as pl
from jax.experimental.pallas import tpu as pltpu

F32 = jnp.float32
BF16 = jnp.bfloat16

HEAD_DIM = 128
N_HEADS_A = 16
N_HEADS_B = 16
CHUNK = 64
LEFT_CHUNKS = 8
REL_CLIP = 256
N_EXPERTS = 32
TOP_K = 4
TOP_K_SHIFT = 2
assert 1 << TOP_K_SHIFT == TOP_K
SWIGLU_ALPHA = 1.702
SWIGLU_LIMIT = 7.0
EPS = 1e-5
NEG = -1e30

V7X_LANES = 128
V7X_VMEM_BYTES = 64 * 1024 * 1024
VMEM_LIMIT = V7X_VMEM_BYTES - 4 * 1024 * 1024


def _cparams(n_axes):
    return pltpu.CompilerParams(
        dimension_semantics=("arbitrary",) * n_axes, vmem_limit_bytes=VMEM_LIMIT)


def _rmsnorm_kernel(x_ref, g_ref, o_ref):
    x = x_ref[...]
    ms = jnp.mean(x * x, axis=-1, keepdims=True)
    o_ref[...] = (x * lax.rsqrt(ms + EPS) * g_ref[...]).astype(o_ref.dtype)


def _rmsnorm(x, g, *, tm, out_dtype):
    t, d = x.shape
    return pl.pallas_call(
        _rmsnorm_kernel,
        grid=(t // tm,),
        in_specs=[pl.BlockSpec((tm, d), lambda i: (i, 0)),
                  pl.BlockSpec((1, d), lambda i: (0, 0))],
        out_specs=pl.BlockSpec((tm, d), lambda i: (i, 0)),
        out_shape=jax.ShapeDtypeStruct((t, d), out_dtype),
        compiler_params=_cparams(1),
        name="rmsnorm",
    )(x, g.reshape(1, d))


def _mm_kernel(*refs, has_bias, has_res, act):
    a_ref, w_ref = refs[0], refs[1]
    pos = 2
    b_ref = r_ref = None
    if has_bias:
        b_ref = refs[pos]
        pos += 1
    if has_res:
        r_ref = refs[pos]
        pos += 1
    o_ref = refs[pos]

    acc = jnp.dot(a_ref[...], w_ref[...].astype(BF16), preferred_element_type=F32)
    if has_bias:
        acc = acc + b_ref[...]
    if act == "sigmoid":
        acc = jax.nn.sigmoid(acc)
    if has_res:
        acc = acc + r_ref[...]
    o_ref[...] = acc.astype(o_ref.dtype)


def _mm(a, w, *, col_off, n_out, tm, tn, out_dtype, bias=None, res=None, act=None, name):
    m, k = a.shape
    assert w.shape[0] == k and col_off % tn == 0 and n_out % tn == 0 and m % tm == 0
    off = col_off // tn
    in_specs = [pl.BlockSpec((tm, k), lambda j, i: (i, 0)),
                pl.BlockSpec((k, tn), lambda j, i: (0, j + off))]
    args = [a, w]
    if bias is not None:
        in_specs.append(pl.BlockSpec((1, tn), lambda j, i: (0, j)))
        args.append(bias.reshape(1, n_out))
    if res is not None:
        in_specs.append(pl.BlockSpec((tm, tn), lambda j, i: (i, j)))
        args.append(res)
    kern = functools.partial(_mm_kernel, has_bias=bias is not None, has_res=res is not None, act=act)
    return pl.pallas_call(
        kern,
        grid=(n_out // tn, m // tm),
        in_specs=in_specs,
        out_specs=pl.BlockSpec((tm, tn), lambda j, i: (i, j)),
        out_shape=jax.ShapeDtypeStruct((m, n_out), out_dtype),
        compiler_params=_cparams(2),
        name=name,
    )(*args)


def _branch_kernel(a_ref, b_ref, wa_ref, wb_ref, ga_ref, gb_ref, o_ref):
    ya = jnp.dot(a_ref[...], wa_ref[...].astype(BF16), preferred_element_type=F32)
    yb = jnp.dot(b_ref[...], wb_ref[...].astype(BF16), preferred_element_type=F32)
    o_ref[...] = (ga_ref[...].astype(F32) * ya + gb_ref[...].astype(F32) * yb).astype(o_ref.dtype)


def _branch_merge(att_a, att_b, wa, wb, gates, *, tm, tn):
    m, ka = att_a.shape
    kb = att_b.shape[1]
    d = wa.shape[1]
    nj = d // tn
    return pl.pallas_call(
        _branch_kernel,
        grid=(nj, m // tm),
        in_specs=[pl.BlockSpec((tm, ka), lambda j, i: (i, 0)),
                  pl.BlockSpec((tm, kb), lambda j, i: (i, 0)),
                  pl.BlockSpec((ka, tn), lambda j, i: (0, j)),
                  pl.BlockSpec((kb, tn), lambda j, i: (0, j)),
                  pl.BlockSpec((tm, tn), lambda j, i: (i, j)),
                  pl.BlockSpec((tm, tn), lambda j, i: (i, j + nj))],
        out_specs=pl.BlockSpec((tm, tn), lambda j, i: (i, j)),
        out_shape=jax.ShapeDtypeStruct((m, d), BF16),
        compiler_params=_cparams(2),
        name="branch_merge",
    )(att_a, att_b, wa, wb, gates, gates)


def _chunk_bias_table(rel_bias, tq):
    left = LEFT_CHUNKS * CHUNK
    nk = 3 * tq
    assert left == 2 * tq
    qi = np.arange(tq)[:, None]
    kw = np.arange(nk)[None, :]
    krel = kw - left
    qc = qi // CHUNK
    kc = np.floor_divide(krel, CHUNK)
    valid = (kc >= qc - LEFT_CHUNKS) & (kc <= qc)
    rel_min, rel_max = left - nk + 1, left + tq - 1
    assert -REL_CLIP <= rel_min and rel_max > REL_CLIP
    rb = rel_bias.astype(F32).T
    nh = rb.shape[0]
    by_rel = jnp.concatenate(
        [rb[:, rel_min + REL_CLIP:], jnp.broadcast_to(rb[:, -1:], (nh, rel_max - REL_CLIP))], axis=1)
    period = rel_max - rel_min + 1
    w = jnp.concatenate([by_rel[:, :nk][:, ::-1], by_rel[:, nk:][:, ::-1]], axis=1)
    bias = jnp.tile(w, (1, tq))[:, :tq * (period - 1)].reshape(nh, tq, period - 1)[:, :, :nk]
    return jnp.where(jnp.asarray(valid)[None], bias, NEG)


def _chunk_attn_kernel(q_ref, k_ref, v_ref, bias_ref, o_ref, *, tq, heads, scale):
    i = pl.program_id(2)
    starts = [pl.multiple_of(jnp.maximum(i - 2 + j, 0) * tq, tq) for j in range(3)]
    for h in range(heads):
        hs = slice(h * HEAD_DIM, (h + 1) * HEAD_DIM)
        q = q_ref[:, hs]
        s_blocks = []
        for j in range(3):
            k = k_ref[pl.ds(starts[j], tq), hs]
            s = lax.dot_general(q, k, (((1,), (1,)), ((), ())), preferred_element_type=F32) * scale
            s = s + bias_ref[h, :, j * tq:(j + 1) * tq]
            if j < 2:
                s = jnp.where(i - 2 + j >= 0, s, NEG)
            s_blocks.append(s)
        m = jnp.maximum(jnp.maximum(jnp.max(s_blocks[0], axis=-1, keepdims=True),
                                    jnp.max(s_blocks[1], axis=-1, keepdims=True)),
                        jnp.max(s_blocks[2], axis=-1, keepdims=True))
        l = jnp.zeros_like(m)
        acc = jnp.zeros((tq, HEAD_DIM), F32)
        for j in range(3):
            p = jnp.exp(s_blocks[j] - m)
            l = l + jnp.sum(p, axis=-1, keepdims=True)
            v = v_ref[pl.ds(starts[j], tq), hs]
            acc = acc + jnp.dot(p.astype(BF16), v, preferred_element_type=F32)
        o_ref[:, hs] = (acc / l).astype(o_ref.dtype)


def _chunk_attention(qkv, bias_tab, *, batch, seq, tq, heads):
    nq = seq // tq
    nh = N_HEADS_A
    assert nh % heads == 0
    ng = nh // heads
    width = heads * HEAD_DIM
    kern = functools.partial(_chunk_attn_kernel, tq=tq, heads=heads, scale=1.0 / math.sqrt(HEAD_DIM))
    return pl.pallas_call(
        kern,
        grid=(batch, ng, nq),
        in_specs=[pl.BlockSpec((tq, width), lambda b, g, i: (b * nq + i, g)),
                  pl.BlockSpec((seq, width), lambda b, g, i: (b, ng + g)),
                  pl.BlockSpec((seq, width), lambda b, g, i: (b, 2 * ng + g)),
                  pl.BlockSpec((heads, tq, 3 * tq), lambda b, g, i: (g, 0, 0))],
        out_specs=pl.BlockSpec((tq, width), lambda b, g, i: (b * nq + i, g)),
        out_shape=jax.ShapeDtypeStruct((batch * seq, nh * HEAD_DIM), BF16),
        compiler_params=_cparams(3),
        name="chunk_attention",
    )(qkv, qkv, qkv, bias_tab)


def _suffix_matrix(tk):
    j = np.arange(tk)[:, None]
    s = np.arange(tk)[None, :]
    later = -(j > s).astype(np.float32)
    half = np.concatenate([later, -np.ones((tk, tk), np.float32)], axis=1)
    return jnp.asarray(np.concatenate([half, half], axis=0), BF16)


def _softplus(z):
    return jnp.maximum(z, 0.0) + jnp.log(1.0 + jnp.exp(-jnp.abs(z)))


def _stick_kernel(q_ref, k_ref, v_ref, u_ref, o_ref, acc_ref, run_ref, *, tq, tk, heads, scale):
    i = pl.program_id(2)
    u = u_ref[...]
    row = lax.broadcasted_iota(jnp.int32, (tq, tq), 0)
    col = lax.broadcasted_iota(jnp.int32, (tq, tq), 1)
    causal = col < row

    def span(h, start, diagonal):
        hs = slice(h * HEAD_DIM, (h + 1) * HEAD_DIM)
        q = q_ref[:, hs]
        k = k_ref[pl.ds(start, tq), hs]
        v = v_ref[pl.ds(start, tq), hs]
        z = lax.dot_general(q, k, (((1,), (1,)), ((), ())), preferred_element_type=F32) * scale
        sp = _softplus(z)
        neg_log_keep = jnp.where(causal, sp, 0.0) if diagonal else sp
        run = run_ref[h]
        later = [None, None]
        for half in (1, 0):
            s = neg_log_keep[:, half * tk:(half + 1) * tk]
            hi = s.astype(BF16)
            lo = (s - hi.astype(F32)).astype(BF16)
            sums = jnp.dot(jnp.concatenate([hi, lo], axis=1), u, preferred_element_type=F32)
            later[half] = sums[:, :tk] + run
            run = run + sums[:, tk:]
        run_ref[h] = run
        a = jnp.exp(z - sp + jnp.concatenate(later, axis=1))
        if diagonal:
            a = jnp.where(causal, a, 0.0)
        acc_ref[h] += jnp.dot(a.astype(BF16), v, preferred_element_type=F32)

    acc_ref[...] = jnp.zeros_like(acc_ref)
    run_ref[...] = jnp.zeros_like(run_ref)
    for h in range(heads):
        span(h, pl.multiple_of(i * tq, tq), True)

    def body(t, c):
        start = pl.multiple_of((i - 1 - t) * tq, tq)
        for h in range(heads):
            span(h, start, False)
        return c

    lax.fori_loop(0, i, body, 0)
    for h in range(heads):
        o_ref[:, h * HEAD_DIM:(h + 1) * HEAD_DIM] = acc_ref[h].astype(o_ref.dtype)


def _stick_attention(qkv, *, batch, seq, tq, heads, col_off):
    nq = seq // tq
    nh = N_HEADS_B
    tk = tq // 2
    width = heads * HEAD_DIM
    assert col_off % width == 0 and nh % heads == 0
    c0 = col_off // width
    ng = nh // heads
    kern = functools.partial(_stick_kernel, tq=tq, tk=tk, heads=heads, scale=1.0 / math.sqrt(HEAD_DIM))
    return pl.pallas_call(
        kern,
        grid=(batch, ng, nq),
        in_specs=[pl.BlockSpec((tq, width), lambda b, g, i: (b * nq + i, c0 + g)),
                  pl.BlockSpec((seq, width), lambda b, g, i: (b, c0 + ng + g)),
                  pl.BlockSpec((seq, width), lambda b, g, i: (b, c0 + 2 * ng + g)),
                  pl.BlockSpec((2 * tk, 2 * tk), lambda b, g, i: (0, 0))],
        out_specs=pl.BlockSpec((tq, width), lambda b, g, i: (b * nq + i, g)),
        out_shape=jax.ShapeDtypeStruct((batch * seq, nh * HEAD_DIM), BF16),
        scratch_shapes=[pltpu.VMEM((heads, tq, HEAD_DIM), F32), pltpu.VMEM((heads, tq, tk), F32)],
        compiler_params=_cparams(3),
        name="stick_attention",
    )(qkv, qkv, qkv, _suffix_matrix(tk))


def _router_kernel(h_ref, g_ref, wr_ref, br_ref, xn_ref, idx_ref, wt_ref, rank_ref, cnt_ref,
                   wr_split, carry, *, tm):
    step = pl.program_id(0)

    @pl.when(step == 0)
    def _():
        w = wr_ref[...]
        hi = w.astype(BF16)
        wr_split[:, :V7X_LANES] = hi
        wr_split[:, V7X_LANES:] = (w - hi.astype(F32)).astype(BF16)
        carry[...] = jnp.zeros_like(carry)

    x = h_ref[...]
    ms = jnp.mean(x * x, axis=-1, keepdims=True)
    xn = x * lax.rsqrt(ms + EPS) * g_ref[...]
    xn_ref[...] = xn.reshape(xn_ref.shape).astype(xn_ref.dtype)
    xh = xn.astype(BF16)
    xl = (xn - xh.astype(F32)).astype(BF16)
    both = jnp.dot(xh, wr_split[...], preferred_element_type=F32)
    logits = (both[:, :V7X_LANES] + both[:, V7X_LANES:]
              + jnp.dot(xl, wr_split[:, :V7X_LANES], preferred_element_type=F32)) + br_ref[...]
    lane = lax.broadcasted_iota(jnp.int32, (tm, V7X_LANES), 1)
    logits = jnp.where(lane < N_EXPERTS, logits, -jnp.inf)

    r = lax.broadcasted_iota(jnp.int32, (tm, tm), 0)
    c = lax.broadcasted_iota(jnp.int32, (tm, tm), 1)
    tri = (c < r).astype(BF16)

    sel_mask = jnp.zeros((tm, V7X_LANES), F32)
    onehots, vals = [], []
    work = logits
    for _ in range(TOP_K):
        mx = jnp.max(work, axis=-1, keepdims=True)
        first = jnp.min(jnp.where(work == mx, lane, V7X_LANES), axis=-1, keepdims=True)
        oh = lane == first
        onehots.append(oh)
        vals.append(mx)
        sel_mask = sel_mask + oh.astype(F32)
        work = jnp.where(oh, -jnp.inf, work)

    prefix = jnp.dot(tri, sel_mask.astype(BF16), preferred_element_type=F32) + carry[...]
    carry[...] = carry[...] + jnp.sum(sel_mask, axis=0, keepdims=True)
    cnt_ref[...] = jnp.broadcast_to(carry[...], cnt_ref.shape)

    es = [jnp.exp(v - vals[0]) for v in vals]
    denom = es[0] + es[1] + es[2] + es[3]
    idx_out = jnp.zeros((tm, V7X_LANES), jnp.int32)
    wt_out = jnp.zeros((tm, V7X_LANES), F32)
    rank_out = jnp.zeros((tm, V7X_LANES), jnp.int32)
    for k in range(TOP_K):
        oh = onehots[k]
        e_k = jnp.sum(jnp.where(oh, lane, 0), axis=-1, keepdims=True)
        r_k = jnp.sum(jnp.where(oh, prefix, 0.0), axis=-1, keepdims=True).astype(jnp.int32)
        idx_out = jnp.where(lane == k, e_k, idx_out)
        wt_out = jnp.where(lane == k, es[k] / denom, wt_out)
        rank_out = jnp.where(lane == k, r_k, rank_out)
    idx_ref[...] = idx_out
    wt_ref[...] = wt_out
    rank_ref[...] = rank_out


def _router(h, g, w_router, b_router, *, tm):
    t, d = h.shape
    wr = jnp.zeros((d, V7X_LANES), F32).at[:, :N_EXPERTS].set(w_router)
    br = jnp.zeros((1, V7X_LANES), F32).at[0, :N_EXPERTS].set(b_router)
    row = lambda i: (i, 0)
    fixed = lambda i: (0, 0)
    kern = functools.partial(_router_kernel, tm=tm)
    return pl.pallas_call(
        kern,
        grid=(t // tm,),
        in_specs=[pl.BlockSpec((tm, d), row),
                  pl.BlockSpec((1, d), fixed),
                  pl.BlockSpec((d, V7X_LANES), fixed),
                  pl.BlockSpec((1, V7X_LANES), fixed)],
        out_specs=[pl.BlockSpec((tm, d // V7X_LANES, V7X_LANES), lambda i: (i, 0, 0)),
                   pl.BlockSpec((tm, V7X_LANES), row),
                   pl.BlockSpec((tm, V7X_LANES), row),
                   pl.BlockSpec((tm, V7X_LANES), row),
                   pl.BlockSpec((8, V7X_LANES), fixed)],
        out_shape=[jax.ShapeDtypeStruct((t, d // V7X_LANES, V7X_LANES), BF16),
                   jax.ShapeDtypeStruct((t, V7X_LANES), jnp.int32),
                   jax.ShapeDtypeStruct((t, V7X_LANES), F32),
                   jax.ShapeDtypeStruct((t, V7X_LANES), jnp.int32),
                   jax.ShapeDtypeStruct((8, V7X_LANES), F32)],
        scratch_shapes=[pltpu.VMEM((d, 2 * V7X_LANES), BF16),
                        pltpu.VMEM((1, V7X_LANES), F32)],
        compiler_params=_cparams(1),
        name="router",
    )(h, g.reshape(1, d), wr, br)


def _row_copy(src_hbm, dst, sem, src_row, dst_row):
    return pltpu.make_async_copy(src_hbm.at[pl.ds(src_row, 1)], dst.at[pl.ds(dst_row, 1)], sem)


def _for_valid_rows(valid, bm, unit, body):
    for rows in range(unit, bm + 1, unit):
        cond = valid <= rows
        if rows > unit:
            cond = jnp.logical_and(cond, valid > rows - unit)
        pl.when(cond)(functools.partial(body, rows))


def _gather_kernel(dest_ref, nu_ref, valid_ref, x_hbm, o_ref, tok_ref, buf, sem, *, bm, unit):
    i = pl.program_id(0)
    n_used = nu_ref[0]

    def issue(block):
        slot = block % 2

        def issue_rows(rows):
            def body(rp, c):
                for p in range(2):
                    r = rp * 2 + p
                    _row_copy(x_hbm, buf.at[slot], sem.at[slot],
                              tok_ref[block * bm + r], r).start(priority=p)
                return c

            lax.fori_loop(0, rows // 2, body, 0, unroll=4)

        _for_valid_rows(valid_ref[block], bm, unit, issue_rows)

    @pl.when(i == 0)
    def _():
        buf[...] = jnp.zeros_like(buf)

        def clear(r, c):
            tok_ref[r] = 0
            return c

        lax.fori_loop(0, tok_ref.shape[0], clear, 0, unroll=8)

        def invert(n, c):
            tok_ref[dest_ref[n]] = lax.shift_right_logical(n, TOP_K_SHIFT)
            return c

        lax.fori_loop(0, dest_ref.shape[0], invert, 0, unroll=8)
        issue(i)

    @pl.when(i + 1 < n_used)
    def _():
        issue(i + 1)

    @pl.when(i < n_used)
    def _():
        slot = i % 2

        def wait_rows(rows):
            pltpu.make_async_copy(x_hbm.at[pl.ds(0, rows)], buf.at[slot, pl.ds(0, rows)],
                                  sem.at[slot]).wait()

        _for_valid_rows(valid_ref[i], bm, unit, wait_rows)
        o_ref[...] = buf[slot].reshape(o_ref.shape)

    @pl.when(i >= n_used)
    def _():
        o_ref[...] = jnp.zeros_like(o_ref)


def _gather_rows(dest, n_used, valid_rows, xn, *, bm, unit):
    n_rows = valid_rows.shape[0] * bm
    _, sub, lanes = xn.shape
    d = sub * lanes
    kern = functools.partial(_gather_kernel, bm=bm, unit=unit)
    return pl.pallas_call(
        kern,
        grid_spec=pltpu.PrefetchScalarGridSpec(
            num_scalar_prefetch=3,
            grid=(n_rows // bm,),
            in_specs=[pl.BlockSpec(memory_space=pl.ANY)],
            out_specs=pl.BlockSpec((bm, d), lambda i, dst, nu, vr: (i, 0)),
            scratch_shapes=[pltpu.SMEM((n_rows,), jnp.int32),
                            pltpu.VMEM((2, bm, sub, lanes), xn.dtype), pltpu.SemaphoreType.DMA((2,))]),
        out_shape=jax.ShapeDtypeStruct((n_rows, d), xn.dtype),
        compiler_params=_cparams(1),
        name="gather_rows",
    )(dest, n_used, valid_rows, xn)


def _weight_ring_step(sched, nj, copies):
    be, _, first, group, next_expert, n_groups, _ = sched
    j, i = pl.program_id(0), pl.program_id(1)
    seq = j * n_groups[0] + group[i]
    slot = seq % 2

    @pl.when(first[i] == 1)
    def _():
        @pl.when(seq == 0)
        def _():
            for c in copies(be[i], j, slot):
                c.start(priority=1)

        last_group = group[i] == n_groups[0] - 1

        @pl.when(jnp.logical_not(jnp.logical_and(last_group, j == nj - 1)))
        def _():
            for c in copies(next_expert[i], j + last_group.astype(jnp.int32), 1 - slot):
                c.start(priority=1)

        for c in copies(be[i], j, slot):
            c.wait()

    return slot


N_SCHED = 7


def _expert_up_kernel(*refs, tf, d_ff, nj, unit):
    sched, (xs_ref, w_hbm, bg_ref, bl_ref, h_ref, wbuf, sem) = refs[:N_SCHED], refs[N_SCHED:]
    n_used = sched[1]
    i = pl.program_id(1)

    def copies(e, j, slot):
        col = pl.multiple_of(j * tf, tf)
        return (pltpu.make_async_copy(w_hbm.at[e, :, pl.ds(col, tf)], wbuf.at[slot, 0], sem.at[slot]),
                pltpu.make_async_copy(w_hbm.at[e, :, pl.ds(d_ff + col, tf)], wbuf.at[slot, 1], sem.at[slot]))

    @pl.when(i < n_used[0])
    def _():
        slot = _weight_ring_step(sched, nj, copies)
        bm = xs_ref.shape[0]

        def body(rows):
            x = xs_ref[:rows]
            g = jnp.dot(x, wbuf[slot, 0].astype(BF16), preferred_element_type=F32) + bg_ref[0]
            l = jnp.dot(x, wbuf[slot, 1].astype(BF16), preferred_element_type=F32) + bl_ref[0]
            glu = jnp.minimum(g, SWIGLU_LIMIT)
            lin = jnp.clip(l, -SWIGLU_LIMIT, SWIGLU_LIMIT)
            h_ref[:rows] = (glu * jax.nn.sigmoid(SWIGLU_ALPHA * glu) * (lin + 1.0)).astype(h_ref.dtype)
            if rows < bm:
                h_ref[rows:] = jnp.zeros((bm - rows, h_ref.shape[1]), h_ref.dtype)

        _for_valid_rows(sched[6][i], bm, unit, body)

    @pl.when(i >= n_used[0])
    def _():
        h_ref[...] = jnp.zeros_like(h_ref)


def _expert_up(sched, xs, w_gate_up, b_gate_up, *, bm, unit, tf):
    n_rows, d = xs.shape
    n_e, _, two_ff = w_gate_up.shape
    d_ff = two_ff // 2
    nj = d_ff // tf

    def blk(i, s):
        return jnp.minimum(i, s[1][0] - 1)

    kern = functools.partial(_expert_up_kernel, tf=tf, d_ff=d_ff, nj=nj, unit=unit)
    return pl.pallas_call(
        kern,
        grid_spec=pltpu.PrefetchScalarGridSpec(
            num_scalar_prefetch=len(sched),
            grid=(nj, n_rows // bm),
            in_specs=[pl.BlockSpec((bm, d), lambda j, i, *s: (blk(i, s), 0)),
                      pl.BlockSpec(memory_space=pl.ANY),
                      pl.BlockSpec((1, 1, tf), lambda j, i, *s: (s[0][blk(i, s)], 0, j)),
                      pl.BlockSpec((1, 1, tf), lambda j, i, *s: (s[0][blk(i, s)], 0, nj + j))],
            out_specs=pl.BlockSpec((bm, tf), lambda j, i, *s: (i, j)),
            scratch_shapes=[pltpu.VMEM((2, 2, d, tf), F32), pltpu.SemaphoreType.DMA((2,))]),
        out_shape=jax.ShapeDtypeStruct((n_rows, d_ff), BF16),
        compiler_params=_cparams(2),
        name="expert_up",
    )(*sched, xs, w_gate_up, b_gate_up.reshape(n_e, 1, two_ff), b_gate_up.reshape(n_e, 1, two_ff))


def _expert_down_kernel(*refs, tn, nj, unit):
    sched, (h_ref, w_hbm, b_ref, y_ref, wbuf, sem) = refs[:N_SCHED], refs[N_SCHED:]
    n_used = sched[1]
    i = pl.program_id(1)

    def copies(e, j, slot):
        col = pl.multiple_of(j * tn, tn)
        return (pltpu.make_async_copy(w_hbm.at[e, :, pl.ds(col, tn)], wbuf.at[slot], sem.at[slot]),)

    @pl.when(i < n_used[0])
    def _():
        slot = _weight_ring_step(sched, nj, copies)
        bm = h_ref.shape[0]

        def body(rows):
            y_ref[:rows] = (jnp.dot(h_ref[:rows], wbuf[slot].astype(BF16), preferred_element_type=F32)
                            + b_ref[0])
            if rows < bm:
                y_ref[rows:] = jnp.zeros((bm - rows, y_ref.shape[1]), y_ref.dtype)

        _for_valid_rows(sched[6][i], bm, unit, body)

    @pl.when(i >= n_used[0])
    def _():
        y_ref[...] = jnp.zeros_like(y_ref)


def _expert_down(sched, hidden, w_down, b_down, *, bm, unit, tn):
    n_rows, d_ff = hidden.shape
    n_e, _, d = w_down.shape
    nj = d // tn

    def blk(i, s):
        return jnp.minimum(i, s[1][0] - 1)

    kern = functools.partial(_expert_down_kernel, tn=tn, nj=nj, unit=unit)
    return pl.pallas_call(
        kern,
        grid_spec=pltpu.PrefetchScalarGridSpec(
            num_scalar_prefetch=len(sched),
            grid=(nj, n_rows // bm),
            in_specs=[pl.BlockSpec((bm, d_ff), lambda j, i, *s: (blk(i, s), 0)),
                      pl.BlockSpec(memory_space=pl.ANY),
                      pl.BlockSpec((1, 1, tn), lambda j, i, *s: (s[0][blk(i, s)], 0, j))],
            out_specs=pl.BlockSpec((bm, tn), lambda j, i, *s: (i, j)),
            scratch_shapes=[pltpu.VMEM((2, d_ff, tn), F32), pltpu.SemaphoreType.DMA((2,))]),
        out_shape=jax.ShapeDtypeStruct((n_rows, d), F32),
        compiler_params=_cparams(2),
        name="expert_down",
    )(*sched, hidden, w_down, b_down.reshape(n_e, 1, d))


def _combine_kernel(dest_ref, h_ref, wt_ref, ys_hbm, g_ref, o_ref, buf, sem, *, tm):
    i = pl.program_id(0)

    def issue(tile):
        slot = tile % 2

        def body(r, c):
            for k in range(TOP_K):
                _row_copy(ys_hbm, buf.at[slot, k], sem.at[slot],
                          dest_ref[(tile * tm + r) * TOP_K + k], r).start(priority=k % 2)
            return c

        lax.fori_loop(0, tm, body, 0, unroll=4)

    @pl.when(i == 0)
    def _():
        issue(i)

    @pl.when(i + 1 < pl.num_programs(0))
    def _():
        issue(i + 1)

    slot = i % 2
    for k in range(TOP_K):
        pltpu.make_async_copy(ys_hbm.at[pl.ds(0, tm)], buf.at[slot, k], sem.at[slot]).wait()
    wt = wt_ref[...]
    y = h_ref[...]
    for k in range(TOP_K):
        y = y + wt[:, k:k + 1] * buf[slot, k]
    ms = jnp.mean(y * y, axis=-1, keepdims=True)
    o_ref[...] = y * lax.rsqrt(ms + EPS) * g_ref[...]


def _combine(dest_flat, h, wts, ys, g, *, tm):
    t, d = h.shape
    kern = functools.partial(_combine_kernel, tm=tm)
    return pl.pallas_call(
        kern,
        grid_spec=pltpu.PrefetchScalarGridSpec(
            num_scalar_prefetch=1,
            grid=(t // tm,),
            in_specs=[pl.BlockSpec((tm, d), lambda i, dst: (i, 0)),
                      pl.BlockSpec((tm, V7X_LANES), lambda i, dst: (i, 0)),
                      pl.BlockSpec(memory_space=pl.ANY),
                      pl.BlockSpec((1, d), lambda i, dst: (0, 0))],
            out_specs=pl.BlockSpec((tm, d), lambda i, dst: (i, 0)),
            scratch_shapes=[pltpu.VMEM((2, TOP_K, tm) + ys.shape[1:], ys.dtype),
                            pltpu.SemaphoreType.DMA((2,))]),
        out_shape=jax.ShapeDtypeStruct((t, d), F32),
        compiler_params=_cparams(1),
        name="combine",
    )(dest_flat, h, wts, ys, g.reshape(1, d))


class _Tiles:
    norm_rows = 256
    mm_cols = 1024
    qkv_rows = 1024
    mm_rows = 512
    chunk_q = 256
    chunk_heads = 8
    stick_q = 256
    stick_heads = 8
    router_rows = 256
    moe_block = 512
    moe_unit = 64
    up_cols = 512
    down_cols = 2048
    combine_rows = 256


def _moe_layout(idx, rank, counts, *, bm):
    t = idx.shape[0]
    n_assign = t * TOP_K
    n_blocks = n_assign // bm + N_EXPERTS
    padded = (counts + bm - 1) // bm * bm
    padded_end = jnp.cumsum(padded)
    padded_start = padded_end - padded
    dest = (padded_start[idx] + rank).reshape(-1).astype(jnp.int32)
    blk_start = jnp.arange(n_blocks, dtype=jnp.int32) * bm
    block_expert = jnp.minimum(
        jnp.sum((padded_end[None, :] <= blk_start[:, None]).astype(jnp.int32), axis=1), N_EXPERTS - 1)
    n_used = (padded_end[-1:] // bm).astype(jnp.int32)
    has = padded > 0
    e_ids = jnp.arange(N_EXPERTS, dtype=jnp.int32)
    later = has[None, :] & (e_ids[None, :] > e_ids[:, None])
    first_e = jnp.min(jnp.where(has, e_ids, N_EXPERTS))
    next_later = jnp.min(jnp.where(later, e_ids[None, :], N_EXPERTS), axis=1)
    next_of_expert = jnp.where(next_later < N_EXPERTS, next_later, first_e)
    group_of_expert = jnp.cumsum(has.astype(jnp.int32)) - 1
    used = blk_start < padded_end[-1]
    first = (used & (blk_start == padded_start[block_expert])).astype(jnp.int32)
    real_end = padded_start + counts
    valid_rows = jnp.clip(real_end[block_expert] - blk_start, 0, bm) * used
    sched = (block_expert.astype(jnp.int32), n_used, first,
             group_of_expert[block_expert].astype(jnp.int32),
             next_of_expert[block_expert].astype(jnp.int32),
             jnp.sum(has.astype(jnp.int32)).reshape(1),
             valid_rows.astype(jnp.int32))
    assert len(sched) == N_SCHED
    return dest, sched


def kernel(x, norm_mix_g, w_in, b_gate, rel_bias, w_branch_a, w_branch_b, w_out, norm_ffn_g,
           w_router, b_router, w_gate_up, b_gate_up, w_down, b_down, norm_final_g):
    batch, seq, d = x.shape
    t = batch * seq
    tl = _Tiles
    width_a = N_HEADS_A * HEAD_DIM
    width_b = N_HEADS_B * HEAD_DIM
    qkv_width = 3 * width_a + 3 * width_b
    xt = x.reshape(t, d)

    xn = _rmsnorm(xt, norm_mix_g, tm=tl.norm_rows, out_dtype=BF16)
    qkv = _mm(xn, w_in, col_off=0, n_out=qkv_width, tm=tl.qkv_rows, tn=tl.mm_cols,
              out_dtype=BF16, name="proj_qkv")
    gates = _mm(xn, w_in, col_off=qkv_width, n_out=2 * d, tm=tl.mm_rows, tn=tl.mm_cols,
                out_dtype=BF16, bias=b_gate, act="sigmoid", name="proj_gates")
    att_a = _chunk_attention(qkv, _chunk_bias_table(rel_bias, tl.chunk_q),
                             batch=batch, seq=seq, tq=tl.chunk_q, heads=tl.chunk_heads)
    att_b = _stick_attention(qkv, batch=batch, seq=seq, tq=tl.stick_q, heads=tl.stick_heads,
                             col_off=3 * width_a)
    merged = _branch_merge(att_a, att_b, w_branch_a, w_branch_b, gates, tm=tl.mm_rows, tn=tl.mm_cols)
    h = _mm(merged, w_out, col_off=0, n_out=d, tm=tl.mm_rows, tn=tl.mm_cols,
            out_dtype=F32, res=xt, name="out_proj")

    xn2, idx_l, wt_l, rank_l, cnt = _router(h, norm_ffn_g, w_router, b_router, tm=tl.router_rows)
    counts = cnt[0, :N_EXPERTS].astype(jnp.int32)
    dest, sched = _moe_layout(idx_l[:, :TOP_K], rank_l[:, :TOP_K], counts, bm=tl.moe_block)
    xs = _gather_rows(dest, sched[1], sched[6], xn2, bm=tl.moe_block, unit=tl.moe_unit)
    hidden = _expert_up(sched, xs, w_gate_up, b_gate_up, bm=tl.moe_block, unit=tl.moe_unit, tf=tl.up_cols)
    ys = _expert_down(sched, hidden, w_down, b_down, bm=tl.moe_block, unit=tl.moe_unit, tn=tl.down_cols)
    y = _combine(dest, h, wt_l, ys, norm_final_g, tm=tl.combine_rows)
    return y.reshape(batch, seq, d)
```
